```python
import math
import jax
import jax.numpy as jnp
from jax import lax
import numpy as np

D_MODEL = 2048
BATCH = 2
SEQ = 4096
DEPTH = 4
DEC_BATCH = 16
DEC_SEQ = 32
PAST_LEN = 1024

CHUNK = 64
N_A_LAYERS = DEPTH // 2
N_B_LAYERS = DEPTH - N_A_LAYERS
POOL_WINDOWS = (2, 4, 8, 16)
N_POOL_GROUPS = len(POOL_WINDOWS)
POOL_GROUP_DIM = D_MODEL // N_POOL_GROUPS
POOL_CTX = max(POOL_WINDOWS) - 1
N_HEADS = 32
HEAD_DIM = D_MODEL // N_HEADS
N_LEFT_CHUNKS = 8
KV_WINDOW = N_LEFT_CHUNKS * CHUNK
REL_CLIP = 128
N_EXPERTS = 16
N_EXPERT_GROUPS = 4
EXPERTS_PER_GROUP = N_EXPERTS // N_EXPERT_GROUPS
TOP_K = 2
D_EXPERT = 1024
EXPERT_BLOCK = 128
DEEPNORM_ALPHA = (2 * DEPTH) ** 0.25
DEEPNORM_BETA = (8 * DEPTH) ** -0.25
LN_EPS = 1e-5
NEG_INF = -1e30

kernel_name = 'streaming_pool_chunkattn_grouped_moe'


def _layer_norm(x, g, b):
    xf = x.astype(jnp.float32)
    mu = jnp.mean(xf, axis=-1, keepdims=True)
    var = jnp.mean(jnp.square(xf - mu), axis=-1, keepdims=True)
    y = (xf - mu) * lax.rsqrt(var + LN_EPS) * g.astype(jnp.float32) + b.astype(jnp.float32)
    return y.astype(x.dtype)


def _pool_mixer(xpad, pos, w_pool_l, scale_l):
    L = pos.shape[0]
    xf = xpad.astype(jnp.float32)
    cs = jnp.pad(jnp.cumsum(xf, axis=1), ((0, 0), (1, 0), (0, 0)))
    x_new = xf[:, POOL_CTX:]
    hi = POOL_CTX + 1
    groups = []
    for g, w in enumerate(POOL_WINDOWS):
        c0, c1 = g * POOL_GROUP_DIM, (g + 1) * POOL_GROUP_DIM
        win_sum = cs[:, hi:hi + L, c0:c1] - cs[:, hi - w:hi - w + L, c0:c1]
        count = jnp.minimum(w, pos + 1).astype(jnp.float32)[None, :, None]
        groups.append(win_sum / count - x_new[:, :, c0:c1])
    p = jnp.stack(groups, axis=2)
    y = jnp.einsum('blgc,gcd->blgd', p, w_pool_l.astype(jnp.float32))
    y = y.reshape(xpad.shape[0], L, D_MODEL) * scale_l.astype(jnp.float32)
    return y.astype(xpad.dtype)


def _band_attention(q, k, v, qpos, kpos, rel_bias_l):
    s = jnp.einsum('bqhd,bkhd->bhqk', q, k, preferred_element_type=jnp.float32) * (HEAD_DIM ** -0.5)
    rel = jnp.clip(qpos[:, None] - kpos[None, :], -REL_CLIP, REL_CLIP) + REL_CLIP
    s = s + rel_bias_l.astype(jnp.float32)[:, rel][None]
    qc = qpos // CHUNK
    kc = kpos // CHUNK
    valid = (kpos[None, :] >= 0) & (kc[None, :] <= qc[:, None]) & (kc[None, :] >= qc[:, None] - N_LEFT_CHUNKS)
    s = jnp.where(valid[None, None], s, NEG_INF)
    p = jax.nn.softmax(s, axis=-1)
    return jnp.einsum('bhqk,bkhd->bqhd', p.astype(v.dtype), v)


def _prompt_chunk_attention(q, k, v, rel_bias_l):
    B, S = q.shape[0], q.shape[1]
    n_chunks = S // CHUNK
    pad = N_LEFT_CHUNKS * CHUNK
    band = pad + CHUNK
    kp = jnp.pad(k, ((0, 0), (pad, 0), (0, 0), (0, 0)))
    vp = jnp.pad(v, ((0, 0), (pad, 0), (0, 0), (0, 0)))

    def one_chunk(c):
        start = c * CHUNK
        qb = lax.dynamic_slice_in_dim(q, start, CHUNK, axis=1)
        kb = lax.dynamic_slice_in_dim(kp, start, band, axis=1)
        vb = lax.dynamic_slice_in_dim(vp, start, band, axis=1)
        qpos = start + jnp.arange(CHUNK)
        kpos = start - pad + jnp.arange(band)
        return _band_attention(qb, kb, vb, qpos, kpos, rel_bias_l)

    o = lax.map(one_chunk, jnp.arange(n_chunks))
    return jnp.swapaxes(o, 0, 1).reshape(B, S, N_HEADS, HEAD_DIM)


def _moe(h, w_router, router_bias, w_gate_l, w_up_l, w_down_l):
    B, L, D = h.shape
    T = B * L
    ht = h.reshape(T, D)
    logits = jnp.dot(ht, w_router, preferred_element_type=jnp.float32)
    probs = jax.nn.softmax(logits, axis=-1)
    sel = (probs + router_bias.astype(jnp.float32)).reshape(T, N_EXPERT_GROUPS, EXPERTS_PER_GROUP)
    group_score = jnp.sum(lax.top_k(sel, TOP_K)[0], axis=-1)
    best_g = jnp.argmax(group_score, axis=-1)
    in_group = jnp.take_along_axis(sel, best_g[:, None, None], axis=1)[:, 0]
    _, loc = lax.top_k(in_group, TOP_K)
    expert_idx = best_g[:, None] * EXPERTS_PER_GROUP + loc
    gate = jnp.take_along_axis(probs, expert_idx, axis=1)
    gate = gate / jnp.sum(gate, axis=-1, keepdims=True)

    n_assign = T * TOP_K
    flat_e = expert_idx.reshape(n_assign).astype(jnp.int32)
    flat_tok = (jnp.arange(n_assign) // TOP_K).astype(jnp.int32)
    flat_gate = gate.reshape(n_assign)
    order = jnp.argsort(flat_e)
    se = flat_e[order]
    counts = jnp.zeros((N_EXPERTS,), jnp.int32).at[flat_e].add(1)
    offsets = jnp.cumsum(counts) - counts
    padded = (counts + EXPERT_BLOCK - 1) // EXPERT_BLOCK * EXPERT_BLOCK
    pad_end = jnp.cumsum(padded)
    pad_off = pad_end - padded
    dest = pad_off[se] + (jnp.arange(n_assign) - offsets[se])
    n_blocks = -(-n_assign // EXPERT_BLOCK) + N_EXPERTS
    n_rows = n_blocks * EXPERT_BLOCK
    tok_buf = jnp.full((n_rows,), T, jnp.int32).at[dest].set(flat_tok[order])
    gate_buf = jnp.zeros((n_rows,), jnp.float32).at[dest].set(flat_gate[order])
    hp = jnp.concatenate([ht, jnp.zeros((1, D), ht.dtype)], axis=0)
    x_buf = hp[tok_buf].reshape(n_blocks, EXPERT_BLOCK, D)
    block_e = jnp.minimum(jnp.searchsorted(pad_end, jnp.arange(n_blocks) * EXPERT_BLOCK, side='right'), N_EXPERTS - 1)

    def expert_block(args):
        xb, e = args
        a = jnp.dot(xb, w_gate_l[e])
        u = jnp.dot(xb, w_up_l[e])
        return jnp.dot(jax.nn.silu(a) * u, w_down_l[e])

    y_buf = lax.map(expert_block, (x_buf, block_e)).reshape(n_rows, D)
    contrib = (gate_buf[:, None] * y_buf.astype(jnp.float32)).astype(h.dtype)
    out = jnp.zeros((T + 1, D), h.dtype).at[tok_buf].add(contrib)[:T]
    return out.reshape(B, L, D)


def _trunk(x, pool_ctx, ctx_k, ctx_v, start_pos, w_pool, pool_scale, w_q, w_o, rel_bias, w_k, w_v,
           ln_gain, ln_bias, w_router, router_bias, w_gate, w_up, w_down):
    B, L, _ = x.shape
    pos = start_pos + jnp.arange(L)
    pool_states = []
    k = None
    v = None
    for l in range(DEPTH):
        if l < N_A_LAYERS:
            xpad = jnp.concatenate([pool_ctx[l].astype(x.dtype), x], axis=1)
            pool_states.append(xpad[:, -POOL_CTX:])
            y = _pool_mixer(xpad, pos, w_pool[l], pool_scale[l])
        else:
            j = l - N_A_LAYERS
            q = jnp.dot(x, w_q[j]).reshape(B, L, N_HEADS, HEAD_DIM)
            if ctx_k is None:
                o = _prompt_chunk_attention(q, k, v, rel_bias[j])
            else:
                kv_len = ctx_k.shape[1]
                k_all = jnp.concatenate([ctx_k.astype(k.dtype), k], axis=1)
                v_all = jnp.concatenate([ctx_v.astype(v.dtype), v], axis=1)
                kpos = jnp.concatenate([start_pos - kv_len + jnp.arange(kv_len), pos])
                o = _band_attention(q, k_all, v_all, pos, kpos, rel_bias[j])
            y = jnp.dot(o.reshape(B, L, D_MODEL), w_o[j])
        x = _layer_norm(DEEPNORM_ALPHA * x + y, ln_gain[l, 0], ln_bias[l, 0])
        x = _layer_norm(DEEPNORM_ALPHA * x + _moe(x, w_router, router_bias, w_gate[l], w_up[l], w_down[l]),
                        ln_gain[l, 1], ln_bias[l, 1])
        if l == N_A_LAYERS - 1:
            k = jnp.dot(x, w_k).reshape(B, L, N_HEADS, HEAD_DIM)
            v = jnp.dot(x, w_v).reshape(B, L, N_HEADS, HEAD_DIM)
    return x, jnp.stack(pool_states, axis=0), k, v


def setup_inputs(seed: int = 0) -> dict:
    key = jax.random.key(seed)
    ks = jax.random.split(key, 20)

    def nrm(k, shape, scale):
        return jax.random.normal(k, shape, jnp.float32) * scale

    kv_len = min(KV_WINDOW, PAST_LEN)
    d_inv = D_MODEL ** -0.5
    return {
        'x_prompt': nrm(ks[0], (BATCH, SEQ, D_MODEL), 1.0),
        'x_sample': nrm(ks[1], (DEC_BATCH, DEC_SEQ, D_MODEL), 1.0),
        'state_pool': nrm(ks[2], (N_A_LAYERS, DEC_BATCH, POOL_CTX, D_MODEL), 1.0),
        'cache_k': nrm(ks[3], (DEC_BATCH, kv_len, N_HEADS, HEAD_DIM), 1.0),
        'cache_v': nrm(ks[4], (DEC_BATCH, kv_len, N_HEADS, HEAD_DIM), DEEPNORM_BETA),
        'w_pool': nrm(ks[5], (N_A_LAYERS, N_POOL_GROUPS, POOL_GROUP_DIM, POOL_GROUP_DIM), POOL_GROUP_DIM ** -0.5 * DEEPNORM_BETA),
        'pool_scale': 1.0 + nrm(ks[6], (N_A_LAYERS, D_MODEL), 0.1),
        'w_q': nrm(ks[7], (N_B_LAYERS, D_MODEL, D_MODEL), d_inv),
        'w_o': nrm(ks[8], (N_B_LAYERS, D_MODEL, D_MODEL), d_inv * DEEPNORM_BETA),
        'rel_bias': nrm(ks[9], (N_B_LAYERS, N_HEADS, 2 * REL_CLIP + 1), 0.1),
        'w_k': nrm(ks[10], (D_MODEL, D_MODEL), d_inv),
        'w_v': nrm(ks[11], (D_MODEL, D_MODEL), d_inv * DEEPNORM_BETA),
        'ln_gain': 1.0 + nrm(ks[12], (DEPTH, 2, D_MODEL), 0.05),
        'ln_bias': nrm(ks[13], (DEPTH, 2, D_MODEL), 0.02),
        'w_router': nrm(ks[14], (D_MODEL, N_EXPERTS), d_inv),
        'router_bias': nrm(ks[15], (N_EXPERTS,), 0.01),
        'w_gate': nrm(ks[16], (DEPTH, N_EXPERTS, D_MODEL, D_EXPERT), d_inv),
        'w_up': nrm(ks[17], (DEPTH, N_EXPERTS, D_MODEL, D_EXPERT), d_inv),
        'w_down': nrm(ks[18], (DEPTH, N_EXPERTS, D_EXPERT, D_MODEL), D_EXPERT ** -0.5 * DEEPNORM_BETA),
    }


def reference(x_prompt, x_sample, state_pool, cache_k, cache_v, w_pool, pool_scale, w_q, w_o, rel_bias,
              w_k, w_v, ln_gain, ln_bias, w_router, router_bias, w_gate, w_up, w_down):
    pool_ctx0 = jnp.zeros((N_A_LAYERS, x_prompt.shape[0], POOL_CTX, D_MODEL), x_prompt.dtype)
    y_prompt, pool_p, k_p, v_p = _trunk(x_prompt, pool_ctx0, None, None, 0, w_pool, pool_scale, w_q, w_o,
                                        rel_bias, w_k, w_v, ln_gain, ln_bias, w_router, router_bias,
                                        w_gate, w_up, w_down)
    keep = min(KV_WINDOW, x_prompt.shape[1])
    k_p = k_p[:, -keep:]
    v_p = v_p[:, -keep:]
    y_sample, pool_s, k_s, v_s = _trunk(x_sample, state_pool, cache_k, cache_v, PAST_LEN, w_pool, pool_scale,
                                        w_q, w_o, rel_bias, w_k, w_v, ln_gain, ln_bias, w_router,
                                        router_bias, w_gate, w_up, w_down)
    return (y_prompt, y_sample, pool_p, k_p, v_p, pool_s, k_s, v_s)
```

```python
import functools

import numpy as np
import jax
import jax.numpy as jnp
from jax import lax
from jax.experimental import pallas as pl
from jax.experimental.pallas import tpu as pltpu

D_MODEL = 2048
BATCH = 2
SEQ = 4096
DEPTH = 4
DEC_BATCH = 16
DEC_SEQ = 32
PAST_LEN = 1024
CHUNK = 64
N_A_LAYERS = DEPTH // 2
N_B_LAYERS = DEPTH - N_A_LAYERS
POOL_WINDOWS = (2, 4, 8, 16)
POOL_GROUP_DIM = D_MODEL // len(POOL_WINDOWS)
POOL_CTX = max(POOL_WINDOWS) - 1
N_HEADS = 32
HEAD_DIM = D_MODEL // N_HEADS
N_LEFT_CHUNKS = 8
KV_WINDOW = N_LEFT_CHUNKS * CHUNK
REL_CLIP = 128
N_EXPERTS = 16
N_EXPERT_GROUPS = 4
EXPERTS_PER_GROUP = N_EXPERTS // N_EXPERT_GROUPS
TOP_K = 2
D_EXPERT = 1024
DEEPNORM_ALPHA = (2 * DEPTH) ** 0.25
LN_EPS = 1e-5
NEG_INF = -1e30

T_PROMPT = BATCH * SEQ
T_SAMPLE = DEC_BATCH * DEC_SEQ
T_ALL = T_PROMPT + T_SAMPLE
N_ASSIGN = T_ALL * TOP_K
BAND = KV_WINDOW + CHUNK
KV_CACHE = min(KV_WINDOW, PAST_LEN)
CTX_ROWS = POOL_CTX + 1

LANES = 128
VMEM_LIMIT = 60 * 1024 * 1024

MOE_BM = 256
MOE_SUB = 8
MOE_SBM = MOE_BM * MOE_SUB
MOE_CH = 256
MOE_NC = D_EXPERT // MOE_CH
MOE_NBLK = (N_ASSIGN + N_EXPERTS * (MOE_BM - 1)) // MOE_BM + 1
MOE_ROWS = MOE_NBLK * MOE_BM
MOE_NSB = (MOE_NBLK + (MOE_SUB - 1) * N_EXPERTS) // MOE_SUB

BF16 = jnp.bfloat16
F32 = jnp.float32


def _cparams(sem):
    return pltpu.CompilerParams(dimension_semantics=sem, vmem_limit_bytes=VMEM_LIMIT)


def _layer_norm(v, g, b):
    mu = jnp.mean(v, axis=-1, keepdims=True)
    d = v - mu
    var = jnp.mean(d * d, axis=-1, keepdims=True)
    return d * lax.rsqrt(var + LN_EPS) * g + b


def _argmax_rows(rows):
    best = rows[0]
    idx = jnp.zeros(best.shape, jnp.int32)
    for i in range(1, len(rows)):
        better = rows[i] > best
        idx = jnp.where(better, i, idx)
        best = jnp.where(better, rows[i], best)
    return idx, best


def _select_rows(rows, idx):
    out = rows[-1]
    for i in range(len(rows) - 2, -1, -1):
        out = jnp.where(idx == i, rows[i], out)
    return out


def _route(x1, wr_t, rb):
    logits = lax.dot_general(wr_t, x1, (((1,), (1,)), ((), ())),
                             precision=lax.Precision.HIGHEST,
                             preferred_element_type=F32)
    m = jnp.max(logits, axis=0, keepdims=True)
    ex = jnp.exp(logits - m)
    probs = ex / jnp.sum(ex, axis=0, keepdims=True)
    sel = probs + rb
    srow = [sel[i:i + 1, :] for i in range(N_EXPERTS)]
    prow = [probs[i:i + 1, :] for i in range(N_EXPERTS)]
    gscore = []
    for g in range(N_EXPERT_GROUPS):
        a, b, c, d = srow[4 * g:4 * g + 4]
        hi1, lo1 = jnp.maximum(a, b), jnp.minimum(a, b)
        hi2, lo2 = jnp.maximum(c, d), jnp.minimum(c, d)
        top1 = jnp.maximum(hi1, hi2)
        top2 = jnp.maximum(jnp.minimum(hi1, hi2), jnp.maximum(lo1, lo2))
        gscore.append(top1 + top2)
    bg, _ = _argmax_rows(gscore)
    ig = [_select_rows([srow[4 * g + i] for g in range(N_EXPERT_GROUPS)], bg)
          for i in range(EXPERTS_PER_GROUP)]
    pg = [_select_rows([prow[4 * g + i] for g in range(N_EXPERT_GROUPS)], bg)
          for i in range(EXPERTS_PER_GROUP)]
    l0, _ = _argmax_rows(ig)
    masked = [jnp.where(l0 == i, -jnp.inf, ig[i]) for i in range(EXPERTS_PER_GROUP)]
    l1, _ = _argmax_rows(masked)
    g0 = _select_rows(pg, l0)
    g1 = _select_rows(pg, l1)
    den = g0 + g1
    eidx = jnp.concatenate([bg * EXPERTS_PER_GROUP + l0, bg * EXPERTS_PER_GROUP + l1], axis=0)
    gate = jnp.concatenate([g0 / den, g1 / den], axis=0)
    return eidx, gate


def _finish_mixer(x, y, g_ref, b_ref, wr_ref, rb_ref, x1_ref, e_ref, gt_ref):
    x1 = _layer_norm(DEEPNORM_ALPHA * x + y, g_ref[...], b_ref[...])
    x1_ref[...] = x1
    eidx, gate = _route(x1, wr_ref[...], rb_ref[...])
    e_ref[...] = eidx
    gt_ref[...] = gate


def _pool_tile(i, x_ref, ctx_ref, wp_ref, sc_ref, g_ref, b_ref, wr_ref, rb_ref, x1_ref, e_ref, gt_ref,
               xp, ybuf, *, n_seg, seg_len, tiles_per_seq):
    rows = n_seg * seg_len
    stride = CTX_ROWS + seg_len
    for s in range(n_seg):
        c = ctx_ref[s * CTX_ROWS:(s + 1) * CTX_ROWS, :]
        if tiles_per_seq:
            c = jnp.where(i % tiles_per_seq == 0, 0.0, c)
        xp[s * stride:s * stride + CTX_ROWS, :] = c
        xp[s * stride + CTX_ROWS:(s + 1) * stride, :] = x_ref[s * seg_len:(s + 1) * seg_len, :]
    r = lax.broadcasted_iota(jnp.int32, (rows, 1), 0)
    if tiles_per_seq:
        pos = (i % tiles_per_seq) * rows + r
    else:
        pos = PAST_LEN + r % seg_len
    x = x_ref[...]
    for g, w in enumerate(POOL_WINDOWS):
        c0, c1 = g * POOL_GROUP_DIM, (g + 1) * POOL_GROUP_DIM
        segs = []
        for s in range(n_seg):
            base = s * stride + CTX_ROWS
            acc = xp[base:base + seg_len, c0:c1]
            for j in range(1, w):
                acc = acc + xp[base - j:base - j + seg_len, c0:c1]
            segs.append(acc)
        ws = segs[0] if n_seg == 1 else jnp.concatenate(segs, axis=0)
        cnt = jnp.minimum(w, pos + 1).astype(F32)
        p = ws / cnt - x[:, c0:c1]
        yg = jnp.dot(p.astype(BF16), wp_ref[g], preferred_element_type=F32)
        ybuf[:, c0:c1] = yg * sc_ref[:, c0:c1]
    _finish_mixer(x, ybuf[...], g_ref, b_ref, wr_ref, rb_ref, x1_ref, e_ref, gt_ref)


POOL_TM = 128
POOL_PROMPT_TILES = T_PROMPT // POOL_TM
POOL_SAMPLE_SEGS = POOL_TM // DEC_SEQ


def _pool_kernel(xp_ref, cp_ref, xs_ref, cs_ref, *rest):
    i = pl.program_id(0)

    @pl.when(i < POOL_PROMPT_TILES)
    def _prompt():
        _pool_tile(i, xp_ref, cp_ref, *rest, n_seg=1, seg_len=POOL_TM, tiles_per_seq=SEQ // POOL_TM)

    @pl.when(i >= POOL_PROMPT_TILES)
    def _sample():
        _pool_tile(i, xs_ref, cs_ref, *rest, n_seg=POOL_SAMPLE_SEGS, seg_len=DEC_SEQ, tiles_per_seq=0)


def _pool_layer(x_p, x_s, xs_blk_off, state_l, wp, sc, g, b, wr_t, rb):
    ctx_s = jnp.pad(state_l, ((0, 0), (1, 0), (0, 0))).reshape(DEC_BATCH * CTX_ROWS, D_MODEL)
    ctx_blocks = POOL_TM // CTX_ROWS
    last = POOL_PROMPT_TILES - 1
    full = lambda a: pl.BlockSpec(a.shape, lambda i: (0,) * a.ndim)
    in_specs = [
        pl.BlockSpec((POOL_TM, D_MODEL), lambda i: (jnp.minimum(i, last), 0)),
        pl.BlockSpec((CTX_ROWS, D_MODEL), lambda i: (jnp.maximum(jnp.minimum(i, last) * ctx_blocks - 1, 0), 0)),
        pl.BlockSpec((POOL_TM, D_MODEL), lambda i: (xs_blk_off + jnp.maximum(i - POOL_PROMPT_TILES, 0), 0)),
        pl.BlockSpec((POOL_SAMPLE_SEGS * CTX_ROWS, D_MODEL), lambda i: (jnp.maximum(i - POOL_PROMPT_TILES, 0), 0)),
        full(wp), full(sc), full(g), full(b), full(wr_t), full(rb),
    ]
    out_specs = [
        pl.BlockSpec((POOL_TM, D_MODEL), lambda i: (i, 0)),
        pl.BlockSpec((TOP_K, POOL_TM), lambda i: (0, i)),
        pl.BlockSpec((TOP_K, POOL_TM), lambda i: (0, i)),
    ]
    out_shape = [jax.ShapeDtypeStruct((T_ALL, D_MODEL), F32),
                 jax.ShapeDtypeStruct((TOP_K, T_ALL), jnp.int32),
                 jax.ShapeDtypeStruct((TOP_K, T_ALL), F32)]
    xp_rows = max(CTX_ROWS + POOL_TM, POOL_SAMPLE_SEGS * (CTX_ROWS + DEC_SEQ))
    return pl.pallas_call(
        _pool_kernel, grid=(T_ALL // POOL_TM,), in_specs=in_specs, out_specs=out_specs, out_shape=out_shape,
        scratch_shapes=[pltpu.VMEM((xp_rows, D_MODEL), F32),
                        pltpu.VMEM((POOL_TM, D_MODEL), F32)],
        compiler_params=_cparams(("arbitrary",)),
        name="pool_ln_router")(x_p, x_p, x_s, ctx_s, wp, sc, g, b, wr_t, rb)


def _moe_plan(eidx):
    e = eidx.reshape(N_ASSIGN)
    ids = jnp.arange(N_ASSIGN, dtype=jnp.int32)
    oh = (e[:, None] == jnp.arange(N_EXPERTS, dtype=jnp.int32)[None, :]).astype(jnp.int32)
    csum = jnp.cumsum(oh, axis=0)
    counts = csum[-1]
    rank = jnp.sum(csum * oh, axis=1) - 1
    nblk = (counts + MOE_BM - 1) // MOE_BM
    blk_off = jnp.cumsum(nblk) - nblk
    pos = jnp.sum(oh * blk_off[None, :], axis=1) * MOE_BM + rank
    dst = jnp.zeros((MOE_ROWS,), jnp.int32).at[pos].set(ids)
    src = dst % T_ALL
    nsb = (nblk + MOE_SUB - 1) // MOE_SUB
    sb_end = jnp.cumsum(nsb)
    sb_off = sb_end - nsb
    total = sb_end[-1]
    s = jnp.arange(MOE_NSB, dtype=jnp.int32)
    active = s < total
    s_eff = jnp.minimum(s, total - 1)
    e_s = jnp.minimum(jnp.sum((sb_end[None, :] <= s_eff[:, None]).astype(jnp.int32), axis=1), N_EXPERTS - 1)
    j = s_eff - sb_off[e_s]
    nsub = jnp.where(active, jnp.clip(nblk[e_s] - MOE_SUB * j, 0, MOE_SUB), 0)
    row0 = jnp.where(active, (blk_off[e_s] + MOE_SUB * j) * MOE_BM, 0)
    nvalid = jnp.where(active, jnp.clip(counts[e_s] - MOE_SBM * j, 0, MOE_SBM), 0)
    return (e_s.astype(jnp.int32), row0.astype(jnp.int32), nsub.astype(jnp.int32),
            nvalid.astype(jnp.int32), src.astype(jnp.int32), dst.astype(jnp.int32))


def _moe_kernel(sbe, sbrow0, sbnsub, sbnvalid, srctok, dstrow,
                x_hbm, wg_ref, wu_ref, wd_ref, y_hbm,
                xbf, yacc, wgb, wub, wdb, gsem, ssem):
    s = pl.program_id(0)
    c = pl.program_id(1)
    nsub = sbnsub[s]
    row0 = sbrow0[s]
    nvalid = sbnvalid[s]

    @pl.when(nsub > 0)
    def _step():
        @pl.when(c == 0)
        def _gather():
            def issue(r, carry):
                t = srctok[row0 + r]
                pltpu.make_async_copy(x_hbm.at[pl.ds(t, 1)], yacc.at[pl.ds(r, 1)], gsem).start()
                return carry
            lax.fori_loop(0, nsub * MOE_BM, issue, 0)

            def wait(i, carry):
                rows = pl.ds(pl.multiple_of(i * MOE_BM, MOE_BM), MOE_BM)
                pltpu.make_async_copy(x_hbm.at[pl.ds(0, MOE_BM)], yacc.at[rows], gsem).wait()
                return carry
            lax.fori_loop(0, nsub, wait, 0)

            def convert(i, carry):
                rows = pl.ds(pl.multiple_of(i * MOE_BM, MOE_BM), MOE_BM)
                xbf[rows, :] = yacc[rows, :].astype(BF16)
                return carry
            lax.fori_loop(0, nsub, convert, 0)

        wgb[...] = wg_ref[...].astype(BF16)
        wub[...] = wu_ref[...].astype(BF16)
        wdb[...] = wd_ref[...].astype(BF16)

        def sub(i, carry):
            rows = pl.ds(pl.multiple_of(i * MOE_BM, MOE_BM), MOE_BM)
            x = xbf[rows, :]
            a = jnp.dot(x, wgb[...], preferred_element_type=F32)
            u = jnp.dot(x, wub[...], preferred_element_type=F32)
            h = (a * jax.nn.sigmoid(a)) * u
            y = jnp.dot(h.astype(BF16), wdb[...], preferred_element_type=F32)
            prev = jnp.where(c == 0, 0.0, yacc[rows, :])
            yacc[rows, :] = prev + y
            return carry
        lax.fori_loop(0, nsub, sub, 0)

        @pl.when(c == MOE_NC - 1)
        def _scatter():
            def issue(r, carry):
                d = dstrow[row0 + r]
                pltpu.make_async_copy(yacc.at[pl.ds(r, 1)], y_hbm.at[pl.ds(d, 1)], ssem).start()
                return carry
            lax.fori_loop(0, nvalid, issue, 0)

            def wait(r, carry):
                pltpu.make_async_copy(yacc.at[pl.ds(0, 1)], y_hbm.at[pl.ds(0, 1)], ssem).wait()
                return carry
            lax.fori_loop(0, nvalid, wait, 0)


def _moe_call(x1, plan, wg, wu, wd):
    def chunk(c, nsub, s):
        return jnp.where(nsub[s] > 0, c, MOE_NC - 1)
    grid_spec = pltpu.PrefetchScalarGridSpec(
        num_scalar_prefetch=6,
        grid=(MOE_NSB, MOE_NC),
        in_specs=[
            pl.BlockSpec(memory_space=pl.ANY),
            pl.BlockSpec((None, D_MODEL, MOE_CH), lambda s, c, e, r0, ns, nv, st, dr: (e[s], 0, chunk(c, ns, s))),
            pl.BlockSpec((None, D_MODEL, MOE_CH), lambda s, c, e, r0, ns, nv, st, dr: (e[s], 0, chunk(c, ns, s))),
            pl.BlockSpec((None, MOE_CH, D_MODEL), lambda s, c, e, r0, ns, nv, st, dr: (e[s], chunk(c, ns, s), 0)),
        ],
        out_specs=pl.BlockSpec(memory_space=pl.ANY),
        scratch_shapes=[
            pltpu.VMEM((MOE_SBM, D_MODEL), BF16),
            pltpu.VMEM((MOE_SBM, D_MODEL), F32),
            pltpu.VMEM((D_MODEL, MOE_CH), BF16),
            pltpu.VMEM((D_MODEL, MOE_CH), BF16),
            pltpu.VMEM((MOE_CH, D_MODEL), BF16),
            pltpu.SemaphoreType.DMA,
            pltpu.SemaphoreType.DMA,
        ])
    return pl.pallas_call(
        _moe_kernel, grid_spec=grid_spec,
        out_shape=jax.ShapeDtypeStruct((N_ASSIGN, D_MODEL), F32),
        compiler_params=_cparams(("arbitrary", "arbitrary")),
        name="moe_experts")(*plan, x1, wg, wu, wd)


COMBINE_TM = 256


def _combine_kernel(x_ref, y_ref, gt_ref, g_ref, b_ref, o_ref):
    gt = gt_ref[...]
    m = gt[:, 0:1] * y_ref[0] + gt[:, 1:2] * y_ref[1]
    o_ref[...] = _layer_norm(DEEPNORM_ALPHA * x_ref[...] + m, g_ref[...], b_ref[...])


def _combine_call(x1, y2, gate_t, g, b):
    row = pl.BlockSpec((COMBINE_TM, D_MODEL), lambda i: (i, 0))
    vec = pl.BlockSpec((1, D_MODEL), lambda i: (0, 0))
    return pl.pallas_call(
        _combine_kernel, grid=(T_ALL // COMBINE_TM,),
        in_specs=[row,
                  pl.BlockSpec((TOP_K, COMBINE_TM, D_MODEL), lambda i: (0, i, 0)),
                  pl.BlockSpec((COMBINE_TM, TOP_K), lambda i: (i, 0)),
                  vec, vec],
        out_specs=row,
        out_shape=jax.ShapeDtypeStruct((T_ALL, D_MODEL), F32),
        compiler_params=_cparams(("arbitrary",)),
        name="combine_ln")(x1, y2.reshape(TOP_K, T_ALL, D_MODEL), gate_t, g, b)


def _moe_layer(x1, eidx, gate, wg, wu, wd, g, b):
    y2 = _moe_call(x1, _moe_plan(eidx), wg, wu, wd)
    return _combine_call(x1, y2, gate.T, g, b)


PROJ_TM = 512


def _proj_kernel(*refs, n_w):
    x = refs[0][...].astype(BF16)
    for k in range(n_w):
        o_ref = refs[1 + n_w + k]
        o_ref[...] = jnp.dot(x, refs[1 + k][...], preferred_element_type=F32).astype(o_ref.dtype)


def _proj_call(x, ws, out_dtype):
    row = pl.BlockSpec((PROJ_TM, D_MODEL), lambda i: (i, 0))
    wspec = pl.BlockSpec((D_MODEL, D_MODEL), lambda i: (0, 0))
    n_w = len(ws)
    return pl.pallas_call(
        functools.partial(_proj_kernel, n_w=n_w), grid=(T_ALL // PROJ_TM,),
        in_specs=[row] + [wspec] * n_w, out_specs=[row] * n_w,
        out_shape=[jax.ShapeDtypeStruct((T_ALL, D_MODEL), out_dtype)] * n_w,
        compiler_params=_cparams(("arbitrary",)),
        name="proj")(x, *ws)


REL_PAD = 384
BIAS_COLS = CHUNK * BAND
BIAS_TILE = 8 * BAND


def _bias_kernel(rb_ref, idx_ref, o_ref):
    r = lax.broadcasted_iota(jnp.int32, (REL_PAD, BIAS_TILE), 0)
    onehot = (r == idx_ref[...]).astype(F32)
    o_ref[...] = jnp.dot(rb_ref[...], onehot, precision=lax.Precision.HIGHEST,
                         preferred_element_type=F32)


def _bias_table(rel_bias):
    q = np.arange(CHUNK)[:, None]
    kb = np.arange(BAND)[None, :]
    idx = (np.clip(q + KV_WINDOW - kb, -REL_CLIP, REL_CLIP) + REL_CLIP).astype(np.int32)
    idx = jnp.asarray(idx.reshape(1, BIAS_COLS))
    rb = jnp.pad(rel_bias, ((0, 0), (0, 0), (0, REL_PAD - rel_bias.shape[-1])))
    out = pl.pallas_call(
        _bias_kernel, grid=(N_B_LAYERS, BIAS_COLS // BIAS_TILE),
        in_specs=[pl.BlockSpec((None, N_HEADS, REL_PAD), lambda j, t: (j, 0, 0)),
                  pl.BlockSpec((1, BIAS_TILE), lambda j, t: (0, t))],
        out_specs=pl.BlockSpec((None, N_HEADS, BIAS_TILE), lambda j, t: (j, 0, t)),
        out_shape=jax.ShapeDtypeStruct((N_B_LAYERS, N_HEADS, BIAS_COLS), F32),
        compiler_params=_cparams(("arbitrary", "arbitrary")),
        name="bias_table")(rb, idx)
    return out.reshape(N_B_LAYERS, N_HEADS, CHUNK, BAND)


def _softmax_pv(s, vb):
    mx = jnp.max(s, axis=-1, keepdims=True)
    ex = jnp.exp(s - mx)
    p = ex / jnp.sum(ex, axis=-1, keepdims=True)
    return jnp.dot(p.astype(BF16), vb, preferred_element_type=F32)


def _head_pair_attention(qc, kb, vb, bias0, bias1, valid):
    first = lax.broadcasted_iota(jnp.int32, (1, LANES), 1) < HEAD_DIM
    outs = []
    for h, bias in enumerate((bias0, bias1)):
        qh = jnp.where(first if h == 0 else jnp.logical_not(first), qc, jnp.zeros_like(qc))
        s = lax.dot_general(qh, kb, (((1,), (1,)), ((), ())), preferred_element_type=F32)
        s = s * (HEAD_DIM ** -0.5) + bias
        if valid is not None:
            s = jnp.where(valid, s, NEG_INF)
        outs.append(_softmax_pv(s, vb))
    return jnp.where(first, outs[0], outs[1])


def _attn_prompt_kernel(q_ref, k_ref, v_ref, b_ref, o_ref, kp, vp):
    kp[0:KV_WINDOW, :] = jnp.zeros((KV_WINDOW, LANES), BF16)
    vp[0:KV_WINDOW, :] = jnp.zeros((KV_WINDOW, LANES), BF16)
    kp[KV_WINDOW:, :] = k_ref[...].astype(BF16)
    vp[KV_WINDOW:, :] = v_ref[...].astype(BF16)
    kbi = lax.broadcasted_iota(jnp.int32, (1, BAND), 1)

    def chunk(c, carry):
        r0 = pl.multiple_of(c * CHUNK, CHUNK)
        qc = q_ref[pl.ds(r0, CHUNK), :]
        kb = kp[pl.ds(r0, BAND), :]
        vb = vp[pl.ds(r0, BAND), :]
        valid = kbi >= KV_WINDOW - c * CHUNK
        o = _head_pair_attention(qc, kb, vb, b_ref[0], b_ref[1], valid)
        o_ref[pl.ds(r0, CHUNK), :] = o.astype(o_ref.dtype)
        return carry
    lax.fori_loop(0, SEQ // CHUNK, chunk, 0)


def _attn_prompt_call(q, k, v, table):
    col = pl.BlockSpec((SEQ, LANES), lambda b, hp: (b, hp))
    return pl.pallas_call(
        _attn_prompt_kernel, grid=(BATCH, N_HEADS // 2),
        in_specs=[col, col, col,
                  pl.BlockSpec((2, CHUNK, BAND), lambda b, hp: (hp, 0, 0))],
        out_specs=col,
        out_shape=jax.ShapeDtypeStruct((T_PROMPT, D_MODEL), BF16),
        scratch_shapes=[pltpu.VMEM((KV_WINDOW + SEQ, LANES), BF16),
                        pltpu.VMEM((KV_WINDOW + SEQ, LANES), BF16)],
        compiler_params=_cparams(("arbitrary", "arbitrary")),
        name="attn_prompt")(q, k, v, table)


def _attn_sample_kernel(q_ref, kc_ref, vc_ref, kn_ref, vn_ref, b_ref, o_ref):
    nk = KV_CACHE + DEC_SEQ
    for hp in range(N_HEADS // 2):
        cols = slice(hp * LANES, (hp + 1) * LANES)
        kb = jnp.concatenate([kc_ref[:, cols], kn_ref[:, cols]], axis=0).astype(BF16)
        vb = jnp.concatenate([vc_ref[:, cols], vn_ref[:, cols]], axis=0).astype(BF16)
        o = _head_pair_attention(q_ref[:, cols], kb, vb,
                                 b_ref[2 * hp, 0:DEC_SEQ, 0:nk], b_ref[2 * hp + 1, 0:DEC_SEQ, 0:nk], None)
        o_ref[:, cols] = o.astype(o_ref.dtype)


def _attn_sample_call(q, cache_k, cache_v, k, v, table):
    off = T_PROMPT // DEC_SEQ
    new = pl.BlockSpec((DEC_SEQ, D_MODEL), lambda i: (off + i, 0))
    cache = pl.BlockSpec((None, KV_CACHE, D_MODEL), lambda i: (i, 0, 0))
    return pl.pallas_call(
        _attn_sample_kernel, grid=(DEC_BATCH,),
        in_specs=[new, cache, cache, new, new,
                  pl.BlockSpec((N_HEADS, CHUNK, BAND), lambda i: (0, 0, 0))],
        out_specs=pl.BlockSpec((DEC_SEQ, D_MODEL), lambda i: (i, 0)),
        out_shape=jax.ShapeDtypeStruct((T_SAMPLE, D_MODEL), BF16),
        compiler_params=_cparams(("arbitrary",)),
        name="attn_sample")(q, cache_k, cache_v, k, v, table)


OPROJ_TM = 256


OPROJ_PROMPT_TILES = T_PROMPT // OPROJ_TM


def _oproj_kernel(op_ref, os_ref, wo_ref, x_ref, g_ref, b_ref, wr_ref, rb_ref, x1_ref, e_ref, gt_ref):
    o = jnp.where(pl.program_id(0) < OPROJ_PROMPT_TILES, op_ref[...], os_ref[...])
    y = jnp.dot(o, wo_ref[...], preferred_element_type=F32)
    _finish_mixer(x_ref[...], y, g_ref, b_ref, wr_ref, rb_ref, x1_ref, e_ref, gt_ref)


def _oproj_call(o_p, o_s, wo, x, g, b, wr_t, rb):
    row = pl.BlockSpec((OPROJ_TM, D_MODEL), lambda i: (i, 0))
    row_p = pl.BlockSpec((OPROJ_TM, D_MODEL), lambda i: (jnp.minimum(i, OPROJ_PROMPT_TILES - 1), 0))
    row_s = pl.BlockSpec((OPROJ_TM, D_MODEL), lambda i: (jnp.maximum(i - OPROJ_PROMPT_TILES, 0), 0))
    full = lambda a: pl.BlockSpec(a.shape, lambda i: (0,) * a.ndim)
    rt = pl.BlockSpec((TOP_K, OPROJ_TM), lambda i: (0, i))
    return pl.pallas_call(
        _oproj_kernel, grid=(T_ALL // OPROJ_TM,),
        in_specs=[row_p, row_s, full(wo), row, full(g), full(b), full(wr_t), full(rb)],
        out_specs=[row, rt, rt],
        out_shape=[jax.ShapeDtypeStruct((T_ALL, D_MODEL), F32),
                   jax.ShapeDtypeStruct((TOP_K, T_ALL), jnp.int32),
                   jax.ShapeDtypeStruct((TOP_K, T_ALL), F32)],
        compiler_params=_cparams(("arbitrary",)),
        name="oproj_ln_router")(o_p, o_s, wo, x, g, b, wr_t, rb)


def kernel(x_prompt, x_sample, state_pool, cache_k, cache_v, w_pool, pool_scale, w_q, w_o, rel_bias,
           w_k, w_v, ln_gain, ln_bias, w_router, router_bias, w_gate, w_up, w_down):
    wr_t = w_router.T
    rb = router_bias.reshape(N_EXPERTS, 1)
    wp_bf = w_pool.astype(BF16)
    xp0 = x_prompt.reshape(T_PROMPT, D_MODEL)
    xs0 = x_sample.reshape(T_SAMPLE, D_MODEL)

    pool_p, pool_s = [], []
    x = None
    for l in range(N_A_LAYERS):
        if l == 0:
            x_p, x_s, xs_off = xp0, xs0, 0
            in_p, in_s = x_prompt, x_sample
        else:
            x_p, x_s, xs_off = x, x, POOL_PROMPT_TILES
            in_p = x[:T_PROMPT].reshape(BATCH, SEQ, D_MODEL)
            in_s = x[T_PROMPT:].reshape(DEC_BATCH, DEC_SEQ, D_MODEL)
        pool_p.append(in_p[:, -POOL_CTX:])
        pool_s.append(in_s[:, -POOL_CTX:])
        x1, eidx, gate = _pool_layer(
            x_p, x_s, xs_off, state_pool[l], wp_bf[l], pool_scale[l].reshape(1, D_MODEL),
            ln_gain[l, 0].reshape(1, D_MODEL), ln_bias[l, 0].reshape(1, D_MODEL), wr_t, rb)
        x = _moe_layer(x1, eidx, gate, w_gate[l], w_up[l], w_down[l],
                       ln_gain[l, 1].reshape(1, D_MODEL), ln_bias[l, 1].reshape(1, D_MODEL))

    k, v = _proj_call(x, [w_k.astype(BF16), w_v.astype(BF16)], F32)
    table = _bias_table(rel_bias)
    ck = cache_k.reshape(DEC_BATCH, KV_CACHE, D_MODEL)
    cv = cache_v.reshape(DEC_BATCH, KV_CACHE, D_MODEL)
    for j in range(N_B_LAYERS):
        l = N_A_LAYERS + j
        (q,) = _proj_call(x, [w_q[j].astype(BF16)], BF16)
        o_p = _attn_prompt_call(q, k, v, table[j])
        o_s = _attn_sample_call(q, ck, cv, k, v, table[j])
        x1, eidx, gate = _oproj_call(
            o_p, o_s, w_o[j].astype(BF16), x, ln_gain[l, 0].reshape(1, D_MODEL),
            ln_bias[l, 0].reshape(1, D_MODEL), wr_t, rb)
        x = _moe_layer(x1, eidx, gate, w_gate[l], w_up[l], w_down[l],
                       ln_gain[l, 1].reshape(1, D_MODEL), ln_bias[l, 1].reshape(1, D_MODEL))

    keep = min(KV_WINDOW, SEQ)
    kp = k[:T_PROMPT].reshape(BATCH, SEQ, N_HEADS, HEAD_DIM)[:, -keep:]
    vp = v[:T_PROMPT].reshape(BATCH, SEQ, N_HEADS, HEAD_DIM)[:, -keep:]
    ks = k[T_PROMPT:].reshape(DEC_BATCH, DEC_SEQ, N_HEADS, HEAD_DIM)
    vs = v[T_PROMPT:].reshape(DEC_BATCH, DEC_SEQ, N_HEADS, HEAD_DIM)
    return (x[:T_PROMPT].reshape(BATCH, SEQ, D_MODEL),
            x[T_PROMPT:].reshape(DEC_BATCH, DEC_SEQ, D_MODEL),
            jnp.stack(pool_p, axis=0), kp, vp, jnp.stack(pool_s, axis=0), ks, vs)
```

```python
import functools

import numpy as np
import jax
import jax.numpy as jnp
from jax import lax
from jax.experimental import pallas as pl
from jax.experimental.pallas import tpu as pltpu

D_MODEL = 2048
BATCH = 2
SEQ = 4096
DEPTH = 4
DEC_BATCH = 16
DEC_SEQ = 32
PAST_LEN = 1024
CHUNK = 64
N_A_LAYERS = DEPTH // 2
N_B_LAYERS = DEPTH - N_A_LAYERS
POOL_WINDOWS = (2, 4, 8, 16)
POOL_GROUP_DIM = D_MODEL // len(POOL_WINDOWS)
POOL_CTX = max(POOL_WINDOWS) - 1
N_HEADS = 32
HEAD_DIM = D_MODEL // N_HEADS
N_LEFT_CHUNKS = 8
KV_WINDOW = N_LEFT_CHUNKS * CHUNK
REL_CLIP = 128
N_EXPERTS = 16
N_EXPERT_GROUPS = 4
EXPERTS_PER_GROUP = N_EXPERTS // N_EXPERT_GROUPS
TOP_K = 2
D_EXPERT = 1024
DEEPNORM_ALPHA = (2 * DEPTH) ** 0.25
LN_EPS = 1e-5
NEG_INF = -1e30

T_PROMPT = BATCH * SEQ
T_SAMPLE = DEC_BATCH * DEC_SEQ
T_ALL = T_PROMPT + T_SAMPLE
N_ASSIGN = T_ALL * TOP_K
BAND = KV_WINDOW + CHUNK
KV_CACHE = min(KV_WINDOW, PAST_LEN)
CTX_ROWS = POOL_CTX + 1

LANES = 128
VMEM_LIMIT = 60 * 1024 * 1024

MOE_BM = 256
MOE_SUB = 8
MOE_SBM = MOE_BM * MOE_SUB
MOE_CH = 256
MOE_NC = D_EXPERT // MOE_CH
MOE_NBLK = (N_ASSIGN + N_EXPERTS * (MOE_BM - 1)) // MOE_BM + 1
MOE_ROWS = MOE_NBLK * MOE_BM
MOE_NSB = (MOE_NBLK + (MOE_SUB - 1) * N_EXPERTS) // MOE_SUB

BF16 = jnp.bfloat16
F32 = jnp.float32


def _cparams(sem):
    return pltpu.CompilerParams(dimension_semantics=sem, vmem_limit_bytes=VMEM_LIMIT)


def _layer_norm(v, g, b):
    mu = jnp.mean(v, axis=-1, keepdims=True)
    d = v - mu
    var = jnp.mean(d * d, axis=-1, keepdims=True)
    return d * lax.rsqrt(var + LN_EPS) * g + b


def _argmax_rows(rows):
    best = rows[0]
    idx = jnp.zeros(best.shape, jnp.int32)
    for i in range(1, len(rows)):
        better = rows[i] > best
        idx = jnp.where(better, i, idx)
        best = jnp.where(better, rows[i], best)
    return idx, best


def _select_rows(rows, idx):
    out = rows[-1]
    for i in range(len(rows) - 2, -1, -1):
        out = jnp.where(idx == i, rows[i], out)
    return out


def _route(x1, wr_t, rb):
    logits = lax.dot_general(wr_t.astype(BF16), x1.astype(BF16), (((1,), (1,)), ((), ())),
                             preferred_element_type=F32)
    m = jnp.max(logits, axis=0, keepdims=True)
    ex = jnp.exp(logits - m)
    probs = ex / jnp.sum(ex, axis=0, keepdims=True)
    sel = probs + rb
    srow = [sel[i:i + 1, :] for i in range(N_EXPERTS)]
    prow = [probs[i:i + 1, :] for i in range(N_EXPERTS)]
    gscore = []
    for g in range(N_EXPERT_GROUPS):
        a, b, c, d = srow[4 * g:4 * g + 4]
        hi1, lo1 = jnp.maximum(a, b), jnp.minimum(a, b)
        hi2, lo2 = jnp.maximum(c, d), jnp.minimum(c, d)
        top1 = jnp.maximum(hi1, hi2)
        top2 = jnp.maximum(jnp.minimum(hi1, hi2), jnp.maximum(lo1, lo2))
        gscore.append(top1 + top2)
    bg, _ = _argmax_rows(gscore)
    ig = [_select_rows([srow[4 * g + i] for g in range(N_EXPERT_GROUPS)], bg)
          for i in range(EXPERTS_PER_GROUP)]
    pg = [_select_rows([prow[4 * g + i] for g in range(N_EXPERT_GROUPS)], bg)
          for i in range(EXPERTS_PER_GROUP)]
    l0, _ = _argmax_rows(ig)
    masked = [jnp.where(l0 == i, -jnp.inf, ig[i]) for i in range(EXPERTS_PER_GROUP)]
    l1, _ = _argmax_rows(masked)
    g0 = _select_rows(pg, l0)
    g1 = _select_rows(pg, l1)
    den = g0 + g1
    eidx = jnp.concatenate([bg * EXPERTS_PER_GROUP + l0, bg * EXPERTS_PER_GROUP + l1], axis=0)
    gate = jnp.concatenate([g0 / den, g1 / den], axis=0)
    return eidx, gate


def _finish_mixer(x, y, g_ref, b_ref, wr_ref, rb_ref, x1_ref, e_ref, gt_ref):
    x1 = _layer_norm(DEEPNORM_ALPHA * x + y, g_ref[...], b_ref[...])
    x1_ref[...] = x1
    eidx, gate = _route(x1, wr_ref[...], rb_ref[...])
    e_ref[...] = eidx
    gt_ref[...] = gate


def _pool_tile(i, x_ref, ctx_ref, wp_ref, sc_ref, g_ref, b_ref, wr_ref, rb_ref, x1_ref, e_ref, gt_ref,
               xp, ybuf, *, n_seg, seg_len, tiles_per_seq):
    rows = n_seg * seg_len
    stride = CTX_ROWS + seg_len
    for s in range(n_seg):
        c = ctx_ref[s * CTX_ROWS:(s + 1) * CTX_ROWS, :]
        if tiles_per_seq:
            c = jnp.where(i % tiles_per_seq == 0, 0.0, c)
        xp[s * stride:s * stride + CTX_ROWS, :] = c
        xp[s * stride + CTX_ROWS:(s + 1) * stride, :] = x_ref[s * seg_len:(s + 1) * seg_len, :]
    r = lax.broadcasted_iota(jnp.int32, (rows, 1), 0)
    if tiles_per_seq:
        pos = (i % tiles_per_seq) * rows + r
    else:
        pos = PAST_LEN + r % seg_len
    x = x_ref[...]
    for g, w in enumerate(POOL_WINDOWS):
        c0, c1 = g * POOL_GROUP_DIM, (g + 1) * POOL_GROUP_DIM
        segs = []
        for s in range(n_seg):
            base = s * stride + CTX_ROWS
            acc = xp[base:base + seg_len, c0:c1]
            for j in range(1, w):
                acc = acc + xp[base - j:base - j + seg_len, c0:c1]
            segs.append(acc)
        ws = segs[0] if n_seg == 1 else jnp.concatenate(segs, axis=0)
        cnt = jnp.minimum(w, pos + 1).astype(F32)
        p = ws / cnt - x[:, c0:c1]
        yg = jnp.dot(p.astype(BF16), wp_ref[g], preferred_element_type=F32)
        ybuf[:, c0:c1] = yg * sc_ref[:, c0:c1]
    _finish_mixer(x, ybuf[...], g_ref, b_ref, wr_ref, rb_ref, x1_ref, e_ref, gt_ref)


POOL_TM = 128
POOL_PROMPT_TILES = T_PROMPT // POOL_TM
POOL_SAMPLE_SEGS = POOL_TM // DEC_SEQ


def _pool_kernel(xp_ref, cp_ref, xs_ref, cs_ref, *rest):
    i = pl.program_id(0)

    @pl.when(i < POOL_PROMPT_TILES)
    def _prompt():
        _pool_tile(i, xp_ref, cp_ref, *rest, n_seg=1, seg_len=POOL_TM, tiles_per_seq=SEQ // POOL_TM)

    @pl.when(i >= POOL_PROMPT_TILES)
    def _sample():
        _pool_tile(i, xs_ref, cs_ref, *rest, n_seg=POOL_SAMPLE_SEGS, seg_len=DEC_SEQ, tiles_per_seq=0)


def _pool_layer(x_p, x_s, xs_blk_off, state_l, wp, sc, g, b, wr_t, rb):
    ctx_s = jnp.pad(state_l, ((0, 0), (1, 0), (0, 0))).reshape(DEC_BATCH * CTX_ROWS, D_MODEL)
    ctx_blocks = POOL_TM // CTX_ROWS
    last = POOL_PROMPT_TILES - 1
    full = lambda a: pl.BlockSpec(a.shape, lambda i: (0,) * a.ndim)
    in_specs = [
        pl.BlockSpec((POOL_TM, D_MODEL), lambda i: (jnp.minimum(i, last), 0)),
        pl.BlockSpec((CTX_ROWS, D_MODEL), lambda i: (jnp.maximum(jnp.minimum(i, last) * ctx_blocks - 1, 0), 0)),
        pl.BlockSpec((POOL_TM, D_MODEL), lambda i: (xs_blk_off + jnp.maximum(i - POOL_PROMPT_TILES, 0), 0)),
        pl.BlockSpec((POOL_SAMPLE_SEGS * CTX_ROWS, D_MODEL), lambda i: (jnp.maximum(i - POOL_PROMPT_TILES, 0), 0)),
        full(wp), full(sc), full(g), full(b), full(wr_t), full(rb),
    ]
    out_specs = [
        pl.BlockSpec((POOL_TM, D_MODEL), lambda i: (i, 0)),
        pl.BlockSpec((TOP_K, POOL_TM), lambda i: (0, i)),
        pl.BlockSpec((TOP_K, POOL_TM), lambda i: (0, i)),
    ]
    out_shape = [jax.ShapeDtypeStruct((T_ALL, D_MODEL), F32),
                 jax.ShapeDtypeStruct((TOP_K, T_ALL), jnp.int32),
                 jax.ShapeDtypeStruct((TOP_K, T_ALL), F32)]
    xp_rows = max(CTX_ROWS + POOL_TM, POOL_SAMPLE_SEGS * (CTX_ROWS + DEC_SEQ))
    return pl.pallas_call(
        _pool_kernel, grid=(T_ALL // POOL_TM,), in_specs=in_specs, out_specs=out_specs, out_shape=out_shape,
        scratch_shapes=[pltpu.VMEM((xp_rows, D_MODEL), F32),
                        pltpu.VMEM((POOL_TM, D_MODEL), F32)],
        compiler_params=_cparams(("arbitrary",)),
        name="pool_ln_router")(x_p, x_p, x_s, ctx_s, wp, sc, g, b, wr_t, rb)


def _moe_plan(eidx):
    e = eidx.reshape(N_ASSIGN)
    ids = jnp.arange(N_ASSIGN, dtype=jnp.int32)
    oh = (e[:, None] == jnp.arange(N_EXPERTS, dtype=jnp.int32)[None, :]).astype(jnp.int32)
    csum = jnp.cumsum(oh, axis=0)
    counts = csum[-1]
    rank = jnp.sum(csum * oh, axis=1) - 1
    nblk = (counts + MOE_BM - 1) // MOE_BM
    blk_off = jnp.cumsum(nblk) - nblk
    pos = jnp.sum(oh * blk_off[None, :], axis=1) * MOE_BM + rank
    spare = N_ASSIGN + jnp.arange(MOE_ROWS, dtype=jnp.int32) % MOE_BM
    dst = spare.at[pos].set(ids)
    src = jnp.where(dst < N_ASSIGN, dst % T_ALL, 0)
    nsb = (nblk + MOE_SUB - 1) // MOE_SUB
    sb_end = jnp.cumsum(nsb)
    sb_off = sb_end - nsb
    total = sb_end[-1]
    s = jnp.arange(MOE_NSB, dtype=jnp.int32)
    active = s < total
    s_eff = jnp.minimum(s, total - 1)
    e_s = jnp.minimum(jnp.sum((sb_end[None, :] <= s_eff[:, None]).astype(jnp.int32), axis=1), N_EXPERTS - 1)
    j = s_eff - sb_off[e_s]
    nsub = jnp.where(active, jnp.clip(nblk[e_s] - MOE_SUB * j, 0, MOE_SUB), 0)
    row0 = jnp.where(active, (blk_off[e_s] + MOE_SUB * j) * MOE_BM, 0)
    return (e_s.astype(jnp.int32), row0.astype(jnp.int32), nsub.astype(jnp.int32),
            src.astype(jnp.int32), dst.astype(jnp.int32))


MOE_DMA_UNROLL = 8


def _moe_kernel(sbe, sbrow0, sbnsub, srctok, dstrow,
                x_hbm, wg_ref, wu_ref, wd_ref, y_hbm,
                xbf, yacc, wgb, wub, wdb, gsem, ssem):
    s = pl.program_id(0)
    c = pl.program_id(1)
    nsub = sbnsub[s]
    row0 = sbrow0[s]
    n_groups = nsub * (MOE_BM // MOE_DMA_UNROLL)

    def block_rows(i):
        return pl.ds(pl.multiple_of(i * MOE_BM, MOE_BM), MOE_BM)

    @pl.when(jnp.logical_and(s == 0, c == 0))
    def _init_spare_rows():
        yacc[0:MOE_BM, :] = jnp.zeros((MOE_BM, D_MODEL), F32)
        spare = pltpu.make_async_copy(yacc.at[pl.ds(0, MOE_BM)], y_hbm.at[pl.ds(N_ASSIGN, MOE_BM)], ssem)
        spare.start()
        spare.wait()

    @pl.when(nsub > 0)
    def _step():
        @pl.when(c == 0)
        def _gather():
            def issue(g, carry):
                for j in range(MOE_DMA_UNROLL):
                    r = g * MOE_DMA_UNROLL + j
                    t = srctok[row0 + r]
                    pltpu.make_async_copy(x_hbm.at[pl.ds(t, 1)], yacc.at[pl.ds(r, 1)], gsem).start()
                return carry
            lax.fori_loop(0, n_groups, issue, 0)

            def wait(i, carry):
                pltpu.make_async_copy(x_hbm.at[pl.ds(0, MOE_BM)], yacc.at[block_rows(i)], gsem).wait()
                return carry
            lax.fori_loop(0, nsub, wait, 0)

            def convert(i, carry):
                rows = block_rows(i)
                xbf[rows, :] = yacc[rows, :].astype(BF16)
                return carry
            lax.fori_loop(0, nsub, convert, 0)

        wgb[...] = wg_ref[...].astype(BF16)
        wub[...] = wu_ref[...].astype(BF16)
        wdb[...] = wd_ref[...].astype(BF16)

        def sub(i, carry):
            rows = block_rows(i)
            x = xbf[rows, :]
            a = jnp.dot(x, wgb[...], preferred_element_type=F32)
            u = jnp.dot(x, wub[...], preferred_element_type=F32)
            h = (a * jax.nn.sigmoid(a)) * u
            y = jnp.dot(h.astype(BF16), wdb[...], preferred_element_type=F32)
            prev = jnp.where(c == 0, 0.0, yacc[rows, :])
            yacc[rows, :] = prev + y
            return carry
        lax.fori_loop(0, nsub, sub, 0)

        @pl.when(c == MOE_NC - 1)
        def _scatter():
            def issue(g, carry):
                for j in range(MOE_DMA_UNROLL):
                    r = g * MOE_DMA_UNROLL + j
                    d = dstrow[row0 + r]
                    pltpu.make_async_copy(yacc.at[pl.ds(r, 1)], y_hbm.at[pl.ds(d, 1)], ssem).start()
                return carry
            lax.fori_loop(0, n_groups, issue, 0)

            def wait(i, carry):
                pltpu.make_async_copy(yacc.at[block_rows(i)], y_hbm.at[pl.ds(0, MOE_BM)], ssem).wait()
                return carry
            lax.fori_loop(0, nsub, wait, 0)


def _moe_call(x1, plan, wg, wu, wd, l):
    def chunk(c, nsub, s):
        return jnp.where(nsub[s] > 0, c, MOE_NC - 1)
    grid_spec = pltpu.PrefetchScalarGridSpec(
        num_scalar_prefetch=5,
        grid=(MOE_NSB, MOE_NC),
        in_specs=[
            pl.BlockSpec(memory_space=pl.ANY),
            pl.BlockSpec((None, None, D_MODEL, MOE_CH),
                         lambda s, c, e, r0, ns, st, dr: (l, e[s], 0, chunk(c, ns, s))),
            pl.BlockSpec((None, None, D_MODEL, MOE_CH),
                         lambda s, c, e, r0, ns, st, dr: (l, e[s], 0, chunk(c, ns, s))),
            pl.BlockSpec((None, None, MOE_CH, D_MODEL),
                         lambda s, c, e, r0, ns, st, dr: (l, e[s], chunk(c, ns, s), 0)),
        ],
        out_specs=pl.BlockSpec(memory_space=pl.ANY),
        scratch_shapes=[
            pltpu.VMEM((MOE_SBM, D_MODEL), BF16),
            pltpu.VMEM((MOE_SBM, D_MODEL), F32),
            pltpu.VMEM((D_MODEL, MOE_CH), BF16),
            pltpu.VMEM((D_MODEL, MOE_CH), BF16),
            pltpu.VMEM((MOE_CH, D_MODEL), BF16),
            pltpu.SemaphoreType.DMA,
            pltpu.SemaphoreType.DMA,
        ])
    return pl.pallas_call(
        _moe_kernel, grid_spec=grid_spec,
        out_shape=jax.ShapeDtypeStruct((N_ASSIGN + MOE_BM, D_MODEL), F32),
        compiler_params=_cparams(("arbitrary", "arbitrary")),
        name="moe_experts")(*plan, x1, wg, wu, wd)


COMBINE_TM = 256


COMBINE_PROMPT_TILES = T_PROMPT // COMBINE_TM


def _combine_kernel(x_ref, y0_ref, y1_ref, gt_ref, g_ref, b_ref, *o_refs):
    gt = gt_ref[...]
    m = gt[:, 0:1] * y0_ref[...] + gt[:, 1:2] * y1_ref[...]
    out = _layer_norm(DEEPNORM_ALPHA * x_ref[...] + m, g_ref[...], b_ref[...])
    if len(o_refs) == 1:
        o_refs[0][...] = out
    else:
        i = pl.program_id(0)

        @pl.when(i < COMBINE_PROMPT_TILES)
        def _prompt():
            o_refs[0][...] = out

        @pl.when(i >= COMBINE_PROMPT_TILES)
        def _sample():
            o_refs[1][...] = out


def _combine_call(x1, y2, gate_t, g, b, split):
    row = pl.BlockSpec((COMBINE_TM, D_MODEL), lambda i: (i, 0))
    vec = pl.BlockSpec((1, D_MODEL), lambda i: (0, 0))
    if split:
        out_specs = [
            pl.BlockSpec((COMBINE_TM, D_MODEL), lambda i: (jnp.minimum(i, COMBINE_PROMPT_TILES - 1), 0)),
            pl.BlockSpec((COMBINE_TM, D_MODEL), lambda i: (jnp.maximum(i - COMBINE_PROMPT_TILES, 0), 0))]
        out_shape = [jax.ShapeDtypeStruct((T_PROMPT, D_MODEL), F32),
                     jax.ShapeDtypeStruct((T_SAMPLE, D_MODEL), F32)]
    else:
        out_specs = row
        out_shape = jax.ShapeDtypeStruct((T_ALL, D_MODEL), F32)
    return pl.pallas_call(
        _combine_kernel, grid=(T_ALL // COMBINE_TM,),
        in_specs=[row, row,
                  pl.BlockSpec((COMBINE_TM, D_MODEL), lambda i: (T_ALL // COMBINE_TM + i, 0)),
                  pl.BlockSpec((COMBINE_TM, TOP_K), lambda i: (i, 0)),
                  vec, vec],
        out_specs=out_specs, out_shape=out_shape,
        compiler_params=_cparams(("arbitrary",)),
        name="combine_ln")(x1, y2, y2, gate_t, g, b)


def _moe_layer(x1, eidx, gate, wg, wu, wd, l, g, b, split=False):
    y2 = _moe_call(x1, _moe_plan(eidx), wg, wu, wd, l)
    return _combine_call(x1, y2, gate.T, g, b, split)


PROJ_TM = 512


def _proj_kernel(*refs, n_w, scale):
    x = refs[0][...].astype(BF16)
    for k in range(n_w):
        o_ref = refs[1 + n_w + k]
        y = jnp.dot(x, refs[1 + k][...], preferred_element_type=F32)
        if scale != 1.0:
            y = y * scale
        o_ref[...] = y.astype(o_ref.dtype)


def _proj_call(x, ws, out_dtype, scale=1.0):
    row = pl.BlockSpec((PROJ_TM, D_MODEL), lambda i: (i, 0))
    wspec = pl.BlockSpec((D_MODEL, D_MODEL), lambda i: (0, 0))
    n_w = len(ws)
    return pl.pallas_call(
        functools.partial(_proj_kernel, n_w=n_w, scale=scale), grid=(T_ALL // PROJ_TM,),
        in_specs=[row] + [wspec] * n_w, out_specs=[row] * n_w,
        out_shape=[jax.ShapeDtypeStruct((T_ALL, D_MODEL), out_dtype)] * n_w,
        compiler_params=_cparams(("arbitrary",)),
        name="proj")(x, *ws)


ATT_TQ = 256
ATT_TK = KV_WINDOW + ATT_TQ
REL_PAD = 384
REL_MASKED = 2 * REL_CLIP + 1
BIAS_COLS = ATT_TQ * ATT_TK
BIAS_TILE = 8 * ATT_TK


def _bias_kernel(rb_ref, idx_ref, o_ref):
    r = lax.broadcasted_iota(jnp.int32, (REL_PAD, BIAS_TILE), 0)
    onehot = (r == idx_ref[...]).astype(F32)
    o_ref[...] = jnp.dot(rb_ref[...], onehot, precision=lax.Precision.HIGHEST,
                         preferred_element_type=F32)


def _bias_table(rel_bias):
    i = np.arange(ATT_TQ)[:, None]
    r = np.arange(ATT_TK)[None, :]
    lo = (i // CHUNK) * CHUNK
    idx = np.clip(i - r + KV_WINDOW, -REL_CLIP, REL_CLIP) + REL_CLIP
    idx = np.where((r >= lo) & (r < lo + BAND), idx, REL_MASKED).astype(np.int32)
    idx = jnp.asarray(idx.reshape(1, BIAS_COLS))
    n_rel = rel_bias.shape[-1]
    rb = jnp.pad(rel_bias, ((0, 0), (0, 0), (0, REL_PAD - n_rel)))
    rb = rb.at[:, :, REL_MASKED].set(NEG_INF)
    out = pl.pallas_call(
        _bias_kernel, grid=(N_B_LAYERS, BIAS_COLS // BIAS_TILE),
        in_specs=[pl.BlockSpec((None, N_HEADS, REL_PAD), lambda j, t: (j, 0, 0)),
                  pl.BlockSpec((1, BIAS_TILE), lambda j, t: (0, t))],
        out_specs=pl.BlockSpec((None, N_HEADS, BIAS_TILE), lambda j, t: (j, 0, t)),
        out_shape=jax.ShapeDtypeStruct((N_B_LAYERS, N_HEADS, BIAS_COLS), F32),
        compiler_params=_cparams(("arbitrary", "arbitrary")),
        name="bias_table")(rb, idx)
    return out.reshape(N_B_LAYERS, N_HEADS, ATT_TQ, ATT_TK)


def _head_pair_attention(qc, kb, vb, bias0, bias1, first_valid):
    first = lax.broadcasted_iota(jnp.int32, (1, LANES), 1) < HEAD_DIM
    outs = []
    for h, bias in enumerate((bias0, bias1)):
        qh = jnp.where(first if h == 0 else jnp.logical_not(first), qc, jnp.zeros_like(qc))
        s = lax.dot_general(qh, kb, (((1,), (1,)), ((), ())), preferred_element_type=F32) + bias
        if first_valid is not None:
            kidx = lax.broadcasted_iota(jnp.int32, (1, s.shape[1]), 1)
            s = jnp.where(kidx >= first_valid, s, NEG_INF)
        ex = jnp.exp(s - jnp.max(s, axis=-1, keepdims=True))
        den = jnp.sum(ex, axis=-1, keepdims=True)
        outs.append(jnp.dot(ex.astype(BF16), vb, preferred_element_type=F32) / den)
    return jnp.where(first, outs[0], outs[1])


ATT_TILES = SEQ // ATT_TQ
ATT_EARLY_TILES = KV_WINDOW // ATT_TQ


def _attn_prompt_kernel(q_ref, k_ref, v_ref, b_ref, o_ref, kp, vp):
    kp[0:KV_WINDOW, :] = jnp.zeros((KV_WINDOW, LANES), BF16)
    vp[0:KV_WINDOW, :] = jnp.zeros((KV_WINDOW, LANES), BF16)
    kp[KV_WINDOW:, :] = k_ref[...].astype(BF16)
    vp[KV_WINDOW:, :] = v_ref[...].astype(BF16)

    def tile(r0, first_valid):
        o = _head_pair_attention(q_ref[pl.ds(r0, ATT_TQ), :], kp[pl.ds(r0, ATT_TK), :],
                                 vp[pl.ds(r0, ATT_TK), :], b_ref[0], b_ref[1], first_valid)
        o_ref[pl.ds(r0, ATT_TQ), :] = o.astype(o_ref.dtype)

    for t in range(ATT_EARLY_TILES):
        tile(t * ATT_TQ, KV_WINDOW - t * ATT_TQ)

    def body(t, carry):
        tile(pl.multiple_of(t * ATT_TQ, ATT_TQ), None)
        return carry
    lax.fori_loop(ATT_EARLY_TILES, ATT_TILES, body, 0)


def _attn_prompt_call(q, k, v, table):
    col = pl.BlockSpec((SEQ, LANES), lambda b, hp: (b, hp))
    return pl.pallas_call(
        _attn_prompt_kernel, grid=(BATCH, N_HEADS // 2),
        in_specs=[col, col, col,
                  pl.BlockSpec((2, ATT_TQ, ATT_TK), lambda b, hp: (hp, 0, 0))],
        out_specs=col,
        out_shape=jax.ShapeDtypeStruct((T_PROMPT, D_MODEL), BF16),
        scratch_shapes=[pltpu.VMEM((KV_WINDOW + SEQ, LANES), BF16),
                        pltpu.VMEM((KV_WINDOW + SEQ, LANES), BF16)],
        compiler_params=_cparams(("arbitrary", "arbitrary")),
        name="attn_prompt")(q, k, v, table)


def _attn_sample_kernel(q_ref, kc_ref, vc_ref, kn_ref, vn_ref, b_ref, o_ref):
    for hp in range(N_HEADS // 2):
        cols = slice(hp * LANES, (hp + 1) * LANES)
        kb = jnp.concatenate([kc_ref[:, cols], kn_ref[:, cols]], axis=0).astype(BF16)
        vb = jnp.concatenate([vc_ref[:, cols], vn_ref[:, cols]], axis=0).astype(BF16)
        o = _head_pair_attention(q_ref[:, cols], kb, vb, b_ref[2 * hp], b_ref[2 * hp + 1], None)
        o_ref[:, cols] = o.astype(o_ref.dtype)


def _attn_sample_call(q, cache_k, cache_v, k, v, table):
    off = T_PROMPT // DEC_SEQ
    new = pl.BlockSpec((DEC_SEQ, D_MODEL), lambda i: (off + i, 0))
    cache = pl.BlockSpec((None, KV_CACHE, D_MODEL), lambda i: (i, 0, 0))
    return pl.pallas_call(
        _attn_sample_kernel, grid=(DEC_BATCH,),
        in_specs=[new, cache, cache, new, new,
                  pl.BlockSpec((N_HEADS, DEC_SEQ, KV_CACHE + DEC_SEQ), lambda i: (0, 0, 0))],
        out_specs=pl.BlockSpec((DEC_SEQ, D_MODEL), lambda i: (i, 0)),
        out_shape=jax.ShapeDtypeStruct((T_SAMPLE, D_MODEL), BF16),
        compiler_params=_cparams(("arbitrary",)),
        name="attn_sample")(q, cache_k, cache_v, k, v, table)


OPROJ_TM = 256


OPROJ_PROMPT_TILES = T_PROMPT // OPROJ_TM


def _oproj_kernel(op_ref, os_ref, wo_ref, x_ref, g_ref, b_ref, wr_ref, rb_ref, x1_ref, e_ref, gt_ref):
    o = jnp.where(pl.program_id(0) < OPROJ_PROMPT_TILES, op_ref[...], os_ref[...])
    y = jnp.dot(o, wo_ref[...], preferred_element_type=F32)
    _finish_mixer(x_ref[...], y, g_ref, b_ref, wr_ref, rb_ref, x1_ref, e_ref, gt_ref)


def _oproj_call(o_p, o_s, wo, x, g, b, wr_t, rb):
    row = pl.BlockSpec((OPROJ_TM, D_MODEL), lambda i: (i, 0))
    row_p = pl.BlockSpec((OPROJ_TM, D_MODEL), lambda i: (jnp.minimum(i, OPROJ_PROMPT_TILES - 1), 0))
    row_s = pl.BlockSpec((OPROJ_TM, D_MODEL), lambda i: (jnp.maximum(i - OPROJ_PROMPT_TILES, 0), 0))
    full = lambda a: pl.BlockSpec(a.shape, lambda i: (0,) * a.ndim)
    rt = pl.BlockSpec((TOP_K, OPROJ_TM), lambda i: (0, i))
    return pl.pallas_call(
        _oproj_kernel, grid=(T_ALL // OPROJ_TM,),
        in_specs=[row_p, row_s, full(wo), row, full(g), full(b), full(wr_t), full(rb)],
        out_specs=[row, rt, rt],
        out_shape=[jax.ShapeDtypeStruct((T_ALL, D_MODEL), F32),
                   jax.ShapeDtypeStruct((TOP_K, T_ALL), jnp.int32),
                   jax.ShapeDtypeStruct((TOP_K, T_ALL), F32)],
        compiler_params=_cparams(("arbitrary",)),
        name="oproj_ln_router")(o_p, o_s, wo, x, g, b, wr_t, rb)


def kernel(x_prompt, x_sample, state_pool, cache_k, cache_v, w_pool, pool_scale, w_q, w_o, rel_bias,
           w_k, w_v, ln_gain, ln_bias, w_router, router_bias, w_gate, w_up, w_down):
    wr_t = w_router.T
    rb = router_bias.reshape(N_EXPERTS, 1)
    wp_bf = w_pool.astype(BF16)
    xp0 = x_prompt.reshape(T_PROMPT, D_MODEL)
    xs0 = x_sample.reshape(T_SAMPLE, D_MODEL)

    pool_p, pool_s = [], []
    x = None
    for l in range(N_A_LAYERS):
        if l == 0:
            x_p, x_s, xs_off = xp0, xs0, 0
            in_p, in_s = x_prompt, x_sample
        else:
            x_p, x_s, xs_off = x, x, POOL_PROMPT_TILES
            in_p = x[:T_PROMPT].reshape(BATCH, SEQ, D_MODEL)
            in_s = x[T_PROMPT:].reshape(DEC_BATCH, DEC_SEQ, D_MODEL)
        pool_p.append(in_p[:, -POOL_CTX:])
        pool_s.append(in_s[:, -POOL_CTX:])
        x1, eidx, gate = _pool_layer(
            x_p, x_s, xs_off, state_pool[l], wp_bf[l], pool_scale[l].reshape(1, D_MODEL),
            ln_gain[l, 0].reshape(1, D_MODEL), ln_bias[l, 0].reshape(1, D_MODEL), wr_t, rb)
        x = _moe_layer(x1, eidx, gate, w_gate, w_up, w_down, l,
                       ln_gain[l, 1].reshape(1, D_MODEL), ln_bias[l, 1].reshape(1, D_MODEL))

    k, v = _proj_call(x, [w_k.astype(BF16), w_v.astype(BF16)], F32)
    table = _bias_table(rel_bias)
    ck = cache_k.reshape(DEC_BATCH, KV_CACHE, D_MODEL)
    cv = cache_v.reshape(DEC_BATCH, KV_CACHE, D_MODEL)
    for j in range(N_B_LAYERS):
        l = N_A_LAYERS + j
        (q,) = _proj_call(x, [w_q[j].astype(BF16)], BF16, scale=HEAD_DIM ** -0.5)
        o_p = _attn_prompt_call(q, k, v, table[j])
        o_s = _attn_sample_call(q, ck, cv, k, v, table[j, :, :DEC_SEQ, :KV_CACHE + DEC_SEQ])
        x1, eidx, gate = _oproj_call(
            o_p, o_s, w_o[j].astype(BF16), x, ln_gain[l, 0].reshape(1, D_MODEL),
            ln_bias[l, 0].reshape(1, D_MODEL), wr_t, rb)
        x = _moe_layer(x1, eidx, gate, w_gate, w_up, w_down, l,
                       ln_gain[l, 1].reshape(1, D_MODEL), ln_bias[l, 1].reshape(1, D_MODEL),
                       split=(l == DEPTH - 1))
    y_p, y_s = x

    keep = min(KV_WINDOW, SEQ)
    kp = k[:T_PROMPT].reshape(BATCH, SEQ, D_MODEL)[:, -keep:].reshape(BATCH, keep, N_HEADS, HEAD_DIM)
    vp = v[:T_PROMPT].reshape(BATCH, SEQ, D_MODEL)[:, -keep:].reshape(BATCH, keep, N_HEADS, HEAD_DIM)
    ks = k[T_PROMPT:].reshape(DEC_BATCH, DEC_SEQ, N_HEADS, HEAD_DIM)
    vs = v[T_PROMPT:].reshape(DEC_BATCH, DEC_SEQ, N_HEADS, HEAD_DIM)
    return (y_p.reshape(BATCH, SEQ, D_MODEL),
            y_s.reshape(DEC_BATCH, DEC_SEQ, D_MODEL),
            jnp.stack(pool_p, axis=0), kp, vp, jnp.stack(pool_s, axis=0), ks, vs)
```

```python
import functools

import numpy as np
import jax
import jax.numpy as jnp
from jax import lax
from jax.experimental import pallas as pl
from jax.experimental.pallas import tpu as pltpu

D_MODEL = 2048
BATCH = 2
SEQ = 4096
DEPTH = 4
DEC_BATCH = 16
DEC_SEQ = 32
PAST_LEN = 1024
CHUNK = 64
N_A_LAYERS = DEPTH // 2
N_B_LAYERS = DEPTH - N_A_LAYERS
POOL_WINDOWS = (2, 4, 8, 16)
POOL_GROUP_DIM = D_MODEL // len(POOL_WINDOWS)
POOL_CTX = max(POOL_WINDOWS) - 1
N_HEADS = 32
HEAD_DIM = D_MODEL // N_HEADS
N_LEFT_CHUNKS = 8
KV_WINDOW = N_LEFT_CHUNKS * CHUNK
REL_CLIP = 128
N_EXPERTS = 16
N_EXPERT_GROUPS = 4
EXPERTS_PER_GROUP = N_EXPERTS // N_EXPERT_GROUPS
TOP_K = 2
D_EXPERT = 1024
DEEPNORM_ALPHA = (2 * DEPTH) ** 0.25
LN_EPS = 1e-5
NEG_INF = -1e30

T_PROMPT = BATCH * SEQ
T_SAMPLE = DEC_BATCH * DEC_SEQ
T_ALL = T_PROMPT + T_SAMPLE
N_ASSIGN = T_ALL * TOP_K
BAND = KV_WINDOW + CHUNK
KV_CACHE = min(KV_WINDOW, PAST_LEN)
CTX_ROWS = POOL_CTX + 1

LANES = 128
VMEM_LIMIT = 60 * 1024 * 1024

MOE_BM = 256
MOE_SUB = 8
MOE_SBM = MOE_BM * MOE_SUB
MOE_CH = 256
MOE_NC = D_EXPERT // MOE_CH
MOE_NBLK = (N_ASSIGN + N_EXPERTS * (MOE_BM - 1)) // MOE_BM + 1
MOE_ROWS = MOE_NBLK * MOE_BM
MOE_NSB = (MOE_NBLK + (MOE_SUB - 1) * N_EXPERTS) // MOE_SUB

BF16 = jnp.bfloat16
F32 = jnp.float32


def _cparams(sem):
    return pltpu.CompilerParams(dimension_semantics=sem, vmem_limit_bytes=VMEM_LIMIT)


def _layer_norm(v, g, b):
    mu = jnp.mean(v, axis=-1, keepdims=True)
    d = v - mu
    var = jnp.mean(d * d, axis=-1, keepdims=True)
    return d * lax.rsqrt(var + LN_EPS) * g + b


def _argmax_rows(rows):
    best = rows[0]
    idx = jnp.zeros(best.shape, jnp.int32)
    for i in range(1, len(rows)):
        better = rows[i] > best
        idx = jnp.where(better, i, idx)
        best = jnp.where(better, rows[i], best)
    return idx, best


def _select_rows(rows, idx):
    out = rows[-1]
    for i in range(len(rows) - 2, -1, -1):
        out = jnp.where(idx == i, rows[i], out)
    return out


def _route(x1, wr_t, rb):
    logits = lax.dot_general(wr_t.astype(BF16), x1.astype(BF16), (((1,), (1,)), ((), ())),
                             preferred_element_type=F32)
    m = jnp.max(logits, axis=0, keepdims=True)
    ex = jnp.exp(logits - m)
    probs = ex / jnp.sum(ex, axis=0, keepdims=True)
    sel = probs + rb
    srow = [sel[i:i + 1, :] for i in range(N_EXPERTS)]
    prow = [probs[i:i + 1, :] for i in range(N_EXPERTS)]
    gscore = []
    for g in range(N_EXPERT_GROUPS):
        a, b, c, d = srow[4 * g:4 * g + 4]
        hi1, lo1 = jnp.maximum(a, b), jnp.minimum(a, b)
        hi2, lo2 = jnp.maximum(c, d), jnp.minimum(c, d)
        top1 = jnp.maximum(hi1, hi2)
        top2 = jnp.maximum(jnp.minimum(hi1, hi2), jnp.maximum(lo1, lo2))
        gscore.append(top1 + top2)
    bg, _ = _argmax_rows(gscore)
    ig = [_select_rows([srow[4 * g + i] for g in range(N_EXPERT_GROUPS)], bg)
          for i in range(EXPERTS_PER_GROUP)]
    pg = [_select_rows([prow[4 * g + i] for g in range(N_EXPERT_GROUPS)], bg)
          for i in range(EXPERTS_PER_GROUP)]
    l0, _ = _argmax_rows(ig)
    masked = [jnp.where(l0 == i, -jnp.inf, ig[i]) for i in range(EXPERTS_PER_GROUP)]
    l1, _ = _argmax_rows(masked)
    g0 = _select_rows(pg, l0)
    g1 = _select_rows(pg, l1)
    den = g0 + g1
    eidx = jnp.concatenate([bg * EXPERTS_PER_GROUP + l0, bg * EXPERTS_PER_GROUP + l1], axis=0)
    gate = jnp.concatenate([g0 / den, g1 / den], axis=0)
    return eidx, gate


def _finish_mixer(x, y, g_ref, b_ref, wr_ref, rb_ref, x1_ref, e_ref, gt_ref):
    x1 = _layer_norm(DEEPNORM_ALPHA * x + y, g_ref[...], b_ref[...])
    x1_ref[...] = x1
    eidx, gate = _route(x1, wr_ref[...], rb_ref[...])
    e_ref[...] = eidx
    gt_ref[...] = gate


def _pool_tile(i, x_ref, ctx_ref, wp_ref, sc_ref, g_ref, b_ref, wr_ref, rb_ref, x1_ref, e_ref, gt_ref,
               xp, ybuf, *, n_seg, seg_len, tiles_per_seq):
    rows = n_seg * seg_len
    stride = CTX_ROWS + seg_len
    for s in range(n_seg):
        c = ctx_ref[s * CTX_ROWS:(s + 1) * CTX_ROWS, :]
        if tiles_per_seq:
            c = jnp.where(i % tiles_per_seq == 0, 0.0, c)
        xp[s * stride:s * stride + CTX_ROWS, :] = c
        xp[s * stride + CTX_ROWS:(s + 1) * stride, :] = x_ref[s * seg_len:(s + 1) * seg_len, :]
    r = lax.broadcasted_iota(jnp.int32, (rows, 1), 0)
    if tiles_per_seq:
        pos = (i % tiles_per_seq) * rows + r
    else:
        pos = PAST_LEN + r % seg_len
    x = x_ref[...]
    for g, w in enumerate(POOL_WINDOWS):
        c0, c1 = g * POOL_GROUP_DIM, (g + 1) * POOL_GROUP_DIM
        segs = []
        for s in range(n_seg):
            base = s * stride + CTX_ROWS
            acc = xp[base:base + seg_len, c0:c1]
            for j in range(1, w):
                acc = acc + xp[base - j:base - j + seg_len, c0:c1]
            segs.append(acc)
        ws = segs[0] if n_seg == 1 else jnp.concatenate(segs, axis=0)
        cnt = jnp.minimum(w, pos + 1).astype(F32)
        p = ws / cnt - x[:, c0:c1]
        yg = jnp.dot(p.astype(BF16), wp_ref[g], preferred_element_type=F32)
        ybuf[:, c0:c1] = yg * sc_ref[:, c0:c1]
    _finish_mixer(x, ybuf[...], g_ref, b_ref, wr_ref, rb_ref, x1_ref, e_ref, gt_ref)


POOL_TM = 128
POOL_PROMPT_TILES = T_PROMPT // POOL_TM
POOL_SAMPLE_SEGS = POOL_TM // DEC_SEQ


def _pool_kernel(xp_ref, cp_ref, xs_ref, cs_ref, *rest):
    i = pl.program_id(0)

    @pl.when(i < POOL_PROMPT_TILES)
    def _prompt():
        _pool_tile(i, xp_ref, cp_ref, *rest, n_seg=1, seg_len=POOL_TM, tiles_per_seq=SEQ // POOL_TM)

    @pl.when(i >= POOL_PROMPT_TILES)
    def _sample():
        _pool_tile(i, xs_ref, cs_ref, *rest, n_seg=POOL_SAMPLE_SEGS, seg_len=DEC_SEQ, tiles_per_seq=0)


def _pool_layer(x_p, x_s, xs_blk_off, state_l, wp, sc, g, b, wr_t, rb):
    ctx_s = jnp.pad(state_l, ((0, 0), (1, 0), (0, 0))).reshape(DEC_BATCH * CTX_ROWS, D_MODEL)
    ctx_blocks = POOL_TM // CTX_ROWS
    last = POOL_PROMPT_TILES - 1
    full = lambda a: pl.BlockSpec(a.shape, lambda i: (0,) * a.ndim)
    in_specs = [
        pl.BlockSpec((POOL_TM, D_MODEL), lambda i: (jnp.minimum(i, last), 0)),
        pl.BlockSpec((CTX_ROWS, D_MODEL), lambda i: (jnp.maximum(jnp.minimum(i, last) * ctx_blocks - 1, 0), 0)),
        pl.BlockSpec((POOL_TM, D_MODEL), lambda i: (xs_blk_off + jnp.maximum(i - POOL_PROMPT_TILES, 0), 0)),
        pl.BlockSpec((POOL_SAMPLE_SEGS * CTX_ROWS, D_MODEL), lambda i: (jnp.maximum(i - POOL_PROMPT_TILES, 0), 0)),
        full(wp), full(sc), full(g), full(b), full(wr_t), full(rb),
    ]
    out_specs = [
        pl.BlockSpec((POOL_TM, D_MODEL), lambda i: (i, 0)),
        pl.BlockSpec((TOP_K, POOL_TM), lambda i: (0, i)),
        pl.BlockSpec((TOP_K, POOL_TM), lambda i: (0, i)),
    ]
    out_shape = [jax.ShapeDtypeStruct((T_ALL, D_MODEL), F32),
                 jax.ShapeDtypeStruct((TOP_K, T_ALL), jnp.int32),
                 jax.ShapeDtypeStruct((TOP_K, T_ALL), F32)]
    xp_rows = max(CTX_ROWS + POOL_TM, POOL_SAMPLE_SEGS * (CTX_ROWS + DEC_SEQ))
    return pl.pallas_call(
        _pool_kernel, grid=(T_ALL // POOL_TM,), in_specs=in_specs, out_specs=out_specs, out_shape=out_shape,
        scratch_shapes=[pltpu.VMEM((xp_rows, D_MODEL), F32),
                        pltpu.VMEM((POOL_TM, D_MODEL), F32)],
        compiler_params=_cparams(("arbitrary",)),
        name="pool_ln_router")(x_p, x_p, x_s, ctx_s, wp, sc, g, b, wr_t, rb)


def _moe_plan(eidx):
    e = eidx.reshape(N_ASSIGN)
    ids = jnp.arange(N_ASSIGN, dtype=jnp.int32)
    oh = (e[:, None] == jnp.arange(N_EXPERTS, dtype=jnp.int32)[None, :]).astype(jnp.int32)
    csum = jnp.cumsum(oh, axis=0)
    counts = csum[-1]
    rank = jnp.sum(csum * oh, axis=1) - 1
    nblk = (counts + MOE_BM - 1) // MOE_BM
    blk_off = jnp.cumsum(nblk) - nblk
    pos = jnp.sum(oh * blk_off[None, :], axis=1) * MOE_BM + rank
    spare = N_ASSIGN + jnp.arange(MOE_ROWS, dtype=jnp.int32) % MOE_BM
    dst = spare.at[pos].set(ids)
    src = jnp.where(dst < N_ASSIGN, dst % T_ALL, 0)
    nsb = (nblk + MOE_SUB - 1) // MOE_SUB
    sb_end = jnp.cumsum(nsb)
    sb_off = sb_end - nsb
    total = sb_end[-1]
    s = jnp.arange(MOE_NSB, dtype=jnp.int32)
    active = s < total
    s_eff = jnp.minimum(s, total - 1)
    e_s = jnp.minimum(jnp.sum((sb_end[None, :] <= s_eff[:, None]).astype(jnp.int32), axis=1), N_EXPERTS - 1)
    j = s_eff - sb_off[e_s]
    nsub = jnp.where(active, jnp.clip(nblk[e_s] - MOE_SUB * j, 0, MOE_SUB), 0)
    row0 = jnp.where(active, (blk_off[e_s] + MOE_SUB * j) * MOE_BM, 0)
    return (e_s.astype(jnp.int32), row0.astype(jnp.int32), nsub.astype(jnp.int32),
            src.astype(jnp.int32), dst.astype(jnp.int32))


MOE_DMA_UNROLL = 8


def _moe_kernel(sbe, sbrow0, sbnsub, srctok, dstrow,
                x_hbm, wg_ref, wu_ref, wd_ref, y_hbm,
                xbf, yacc, wgb, wub, wdb, gsem, ssem):
    s = pl.program_id(0)
    c = pl.program_id(1)
    nsub = sbnsub[s]
    row0 = sbrow0[s]
    n_groups = nsub * (MOE_BM // MOE_DMA_UNROLL)

    def block_rows(i):
        return pl.ds(pl.multiple_of(i * MOE_BM, MOE_BM), MOE_BM)

    @pl.when(jnp.logical_and(s == 0, c == 0))
    def _init_spare_rows():
        yacc[0:MOE_BM, :] = jnp.zeros((MOE_BM, D_MODEL), F32)
        spare = pltpu.make_async_copy(yacc.at[pl.ds(0, MOE_BM)], y_hbm.at[pl.ds(N_ASSIGN, MOE_BM)], ssem)
        spare.start()
        spare.wait()

    @pl.when(nsub > 0)
    def _step():
        @pl.when(c == 0)
        def _gather():
            def issue(g, carry):
                for j in range(MOE_DMA_UNROLL):
                    r = g * MOE_DMA_UNROLL + j
                    t = srctok[row0 + r]
                    pltpu.make_async_copy(x_hbm.at[pl.ds(t, 1)], yacc.at[pl.ds(r, 1)], gsem).start()
                return carry
            lax.fori_loop(0, n_groups, issue, 0)

            def wait(i, carry):
                pltpu.make_async_copy(x_hbm.at[pl.ds(0, MOE_BM)], yacc.at[block_rows(i)], gsem).wait()
                return carry
            lax.fori_loop(0, nsub, wait, 0)

            def convert(i, carry):
                rows = block_rows(i)
                xbf[rows, :] = yacc[rows, :].astype(BF16)
                return carry
            lax.fori_loop(0, nsub, convert, 0)

        wgb[...] = wg_ref[...].astype(BF16)
        wub[...] = wu_ref[...].astype(BF16)
        wdb[...] = wd_ref[...].astype(BF16)

        def sub(i, carry):
            rows = block_rows(i)
            x = xbf[rows, :]
            a = jnp.dot(x, wgb[...], preferred_element_type=F32)
            u = jnp.dot(x, wub[...], preferred_element_type=F32)
            h = (a * jax.nn.sigmoid(a)) * u
            y = jnp.dot(h.astype(BF16), wdb[...], preferred_element_type=F32)
            prev = jnp.where(c == 0, 0.0, yacc[rows, :])
            yacc[rows, :] = prev + y
            return carry
        lax.fori_loop(0, nsub, sub, 0)

        @pl.when(c == MOE_NC - 1)
        def _scatter():
            def issue(g, carry):
                for j in range(MOE_DMA_UNROLL):
                    r = g * MOE_DMA_UNROLL + j
                    d = dstrow[row0 + r]
                    pltpu.make_async_copy(yacc.at[pl.ds(r, 1)], y_hbm.at[pl.ds(d, 1)], ssem).start()
                return carry
            lax.fori_loop(0, n_groups, issue, 0)

            def wait(i, carry):
                pltpu.make_async_copy(yacc.at[block_rows(i)], y_hbm.at[pl.ds(0, MOE_BM)], ssem).wait()
                return carry
            lax.fori_loop(0, nsub, wait, 0)


def _moe_call(x1, plan, wg, wu, wd, l):
    def chunk(c, nsub, s):
        return jnp.where(nsub[s] > 0, c, MOE_NC - 1)
    grid_spec = pltpu.PrefetchScalarGridSpec(
        num_scalar_prefetch=5,
        grid=(MOE_NSB, MOE_NC),
        in_specs=[
            pl.BlockSpec(memory_space=pl.ANY),
            pl.BlockSpec((None, None, D_MODEL, MOE_CH),
                         lambda s, c, e, r0, ns, st, dr: (l, e[s], 0, chunk(c, ns, s))),
            pl.BlockSpec((None, None, D_MODEL, MOE_CH),
                         lambda s, c, e, r0, ns, st, dr: (l, e[s], 0, chunk(c, ns, s))),
            pl.BlockSpec((None, None, MOE_CH, D_MODEL),
                         lambda s, c, e, r0, ns, st, dr: (l, e[s], chunk(c, ns, s), 0)),
        ],
        out_specs=pl.BlockSpec(memory_space=pl.ANY),
        scratch_shapes=[
            pltpu.VMEM((MOE_SBM, D_MODEL), BF16),
            pltpu.VMEM((MOE_SBM, D_MODEL), F32),
            pltpu.VMEM((D_MODEL, MOE_CH), BF16),
            pltpu.VMEM((D_MODEL, MOE_CH), BF16),
            pltpu.VMEM((MOE_CH, D_MODEL), BF16),
            pltpu.SemaphoreType.DMA,
            pltpu.SemaphoreType.DMA,
        ])
    return pl.pallas_call(
        _moe_kernel, grid_spec=grid_spec,
        out_shape=jax.ShapeDtypeStruct((N_ASSIGN + MOE_BM, D_MODEL), F32),
        compiler_params=_cparams(("arbitrary", "arbitrary")),
        name="moe_experts")(*plan, x1, wg, wu, wd)


COMBINE_TM = 256


COMBINE_PROMPT_TILES = T_PROMPT // COMBINE_TM


def _combine_kernel(x_ref, y0_ref, y1_ref, gt_ref, g_ref, b_ref, *o_refs):
    gt = gt_ref[...]
    m = gt[:, 0:1] * y0_ref[...] + gt[:, 1:2] * y1_ref[...]
    out = _layer_norm(DEEPNORM_ALPHA * x_ref[...] + m, g_ref[...], b_ref[...])
    if len(o_refs) == 1:
        o_refs[0][...] = out
    else:
        i = pl.program_id(0)

        @pl.when(i < COMBINE_PROMPT_TILES)
        def _prompt():
            o_refs[0][...] = out

        @pl.when(i >= COMBINE_PROMPT_TILES)
        def _sample():
            o_refs[1][...] = out


def _combine_call(x1, y2, gate_t, g, b, split):
    row = pl.BlockSpec((COMBINE_TM, D_MODEL), lambda i: (i, 0))
    vec = pl.BlockSpec((1, D_MODEL), lambda i: (0, 0))
    if split:
        out_specs = [
            pl.BlockSpec((COMBINE_TM, D_MODEL), lambda i: (jnp.minimum(i, COMBINE_PROMPT_TILES - 1), 0)),
            pl.BlockSpec((COMBINE_TM, D_MODEL), lambda i: (jnp.maximum(i - COMBINE_PROMPT_TILES, 0), 0))]
        out_shape = [jax.ShapeDtypeStruct((T_PROMPT, D_MODEL), F32),
                     jax.ShapeDtypeStruct((T_SAMPLE, D_MODEL), F32)]
    else:
        out_specs = row
        out_shape = jax.ShapeDtypeStruct((T_ALL, D_MODEL), F32)
    return pl.pallas_call(
        _combine_kernel, grid=(T_ALL // COMBINE_TM,),
        in_specs=[row, row,
                  pl.BlockSpec((COMBINE_TM, D_MODEL), lambda i: (T_ALL // COMBINE_TM + i, 0)),
                  pl.BlockSpec((COMBINE_TM, TOP_K), lambda i: (i, 0)),
                  vec, vec],
        out_specs=out_specs, out_shape=out_shape,
        compiler_params=_cparams(("arbitrary",)),
        name="combine_ln")(x1, y2, y2, gate_t, g, b)


def _moe_layer(x1, eidx, gate, wg, wu, wd, l, g, b, split=False):
    y2 = _moe_call(x1, _moe_plan(eidx), wg, wu, wd, l)
    return _combine_call(x1, y2, gate.T, g, b, split)


PROJ_TM = 512


def _proj_kernel(*refs, n_w, scale):
    x = refs[0][...].astype(BF16)
    for k in range(n_w):
        o_ref = refs[1 + n_w + k]
        y = jnp.dot(x, refs[1 + k][...], preferred_element_type=F32)
        if scale != 1.0:
            y = y * scale
        o_ref[...] = y.astype(o_ref.dtype)


def _proj_call(x, ws, out_dtype, scale=1.0):
    row = pl.BlockSpec((PROJ_TM, D_MODEL), lambda i: (i, 0))
    wspec = pl.BlockSpec((D_MODEL, D_MODEL), lambda i: (0, 0))
    n_w = len(ws)
    return pl.pallas_call(
        functools.partial(_proj_kernel, n_w=n_w, scale=scale), grid=(T_ALL // PROJ_TM,),
        in_specs=[row] + [wspec] * n_w, out_specs=[row] * n_w,
        out_shape=[jax.ShapeDtypeStruct((T_ALL, D_MODEL), out_dtype)] * n_w,
        compiler_params=_cparams(("arbitrary",)),
        name="proj")(x, *ws)


ATT_TQ = 256
ATT_TK = KV_WINDOW + ATT_TQ
REL_PAD = 384
REL_MASKED = 2 * REL_CLIP + 1


BIAS_ROWS = 8
BIAS_HEADS = N_B_LAYERS * N_HEADS


def _bias_kernel(rb_ref, idx_ref, o_ref):
    rb = rb_ref[...]
    hi = rb.astype(BF16)
    rest = rb - hi.astype(F32)
    mid = rest.astype(BF16)
    lo = (rest - mid.astype(F32)).astype(BF16)
    r = lax.broadcasted_iota(jnp.int32, (REL_PAD, ATT_TK), 0)
    for i in range(BIAS_ROWS):
        onehot = jnp.where(r == idx_ref[i:i + 1, :], 1.0, 0.0).astype(BF16)
        o_ref[:, i, :] = (jnp.dot(hi, onehot, preferred_element_type=F32)
                          + jnp.dot(mid, onehot, preferred_element_type=F32)
                          + jnp.dot(lo, onehot, preferred_element_type=F32))


def _bias_table(rel_bias):
    i = np.arange(ATT_TQ)[:, None]
    r = np.arange(ATT_TK)[None, :]
    lo = (i // CHUNK) * CHUNK
    idx = np.clip(i - r + KV_WINDOW, -REL_CLIP, REL_CLIP) + REL_CLIP
    idx = jnp.asarray(np.where((r >= lo) & (r < lo + BAND), idx, REL_MASKED).astype(np.int32))
    n_rel = rel_bias.shape[-1]
    rb = jnp.pad(rel_bias, ((0, 0), (0, 0), (0, REL_PAD - n_rel)))
    rb = rb.at[:, :, REL_MASKED].set(NEG_INF).reshape(BIAS_HEADS, REL_PAD)
    return pl.pallas_call(
        _bias_kernel, grid=(ATT_TQ // BIAS_ROWS,),
        in_specs=[pl.BlockSpec((BIAS_HEADS, REL_PAD), lambda t: (0, 0)),
                  pl.BlockSpec((BIAS_ROWS, ATT_TK), lambda t: (t, 0))],
        out_specs=pl.BlockSpec((BIAS_HEADS, BIAS_ROWS, ATT_TK), lambda t: (0, t, 0)),
        out_shape=jax.ShapeDtypeStruct((BIAS_HEADS, ATT_TQ, ATT_TK), F32),
        compiler_params=_cparams(("arbitrary",)),
        name="bias_table")(rb, idx)


def _head_pair_attention(qc, kb, vb, bias0, bias1, first_valid):
    first = lax.broadcasted_iota(jnp.int32, (1, LANES), 1) < HEAD_DIM
    outs = []
    for h, bias in enumerate((bias0, bias1)):
        qh = jnp.where(first if h == 0 else jnp.logical_not(first), qc, jnp.zeros_like(qc))
        s = lax.dot_general(qh, kb, (((1,), (1,)), ((), ())), preferred_element_type=F32) + bias
        if first_valid is not None:
            kidx = lax.broadcasted_iota(jnp.int32, (1, s.shape[1]), 1)
            s = jnp.where(kidx >= first_valid, s, NEG_INF)
        ex = jnp.exp(s - jnp.max(s, axis=-1, keepdims=True))
        den = jnp.sum(ex, axis=-1, keepdims=True)
        outs.append(jnp.dot(ex.astype(BF16), vb, preferred_element_type=F32) / den)
    return jnp.where(first, outs[0], outs[1])


ATT_TILES = SEQ // ATT_TQ
ATT_EARLY_TILES = KV_WINDOW // ATT_TQ


def _attn_prompt_kernel(q_ref, k_ref, v_ref, b_ref, o_ref, kp, vp):
    kp[0:KV_WINDOW, :] = jnp.zeros((KV_WINDOW, LANES), BF16)
    vp[0:KV_WINDOW, :] = jnp.zeros((KV_WINDOW, LANES), BF16)
    kp[KV_WINDOW:, :] = k_ref[...].astype(BF16)
    vp[KV_WINDOW:, :] = v_ref[...].astype(BF16)

    def tile(r0, first_valid):
        o = _head_pair_attention(q_ref[pl.ds(r0, ATT_TQ), :], kp[pl.ds(r0, ATT_TK), :],
                                 vp[pl.ds(r0, ATT_TK), :], b_ref[0], b_ref[1], first_valid)
        o_ref[pl.ds(r0, ATT_TQ), :] = o.astype(o_ref.dtype)

    for t in range(ATT_EARLY_TILES):
        tile(t * ATT_TQ, KV_WINDOW - t * ATT_TQ)

    def body(t, carry):
        tile(pl.multiple_of(2 * t * ATT_TQ, ATT_TQ), None)
        tile(pl.multiple_of((2 * t + 1) * ATT_TQ, ATT_TQ), None)
        return carry
    lax.fori_loop(ATT_EARLY_TILES // 2, ATT_TILES // 2, body, 0)


def _attn_prompt_call(q, k, v, table, j):
    col = pl.BlockSpec((SEQ, LANES), lambda b, hp: (b, hp))
    return pl.pallas_call(
        _attn_prompt_kernel, grid=(BATCH, N_HEADS // 2),
        in_specs=[col, col, col,
                  pl.BlockSpec((2, ATT_TQ, ATT_TK), lambda b, hp: (j * (N_HEADS // 2) + hp, 0, 0))],
        out_specs=col,
        out_shape=jax.ShapeDtypeStruct((T_PROMPT, D_MODEL), BF16),
        scratch_shapes=[pltpu.VMEM((KV_WINDOW + SEQ, LANES), BF16),
                        pltpu.VMEM((KV_WINDOW + SEQ, LANES), BF16)],
        compiler_params=_cparams(("arbitrary", "arbitrary")),
        name="attn_prompt")(q, k, v, table)


def _attn_sample_kernel(q_ref, kc_ref, vc_ref, kn_ref, vn_ref, b_ref, o_ref):
    for hp in range(N_HEADS // 2):
        cols = slice(hp * LANES, (hp + 1) * LANES)
        kb = jnp.concatenate([kc_ref[:, cols], kn_ref[:, cols]], axis=0).astype(BF16)
        vb = jnp.concatenate([vc_ref[:, cols], vn_ref[:, cols]], axis=0).astype(BF16)
        o = _head_pair_attention(q_ref[:, cols], kb, vb, b_ref[2 * hp], b_ref[2 * hp + 1], None)
        o_ref[:, cols] = o.astype(o_ref.dtype)


def _attn_sample_call(q, cache_k, cache_v, k, v, table):
    off = T_PROMPT // DEC_SEQ
    new = pl.BlockSpec((DEC_SEQ, D_MODEL), lambda i: (off + i, 0))
    cache = pl.BlockSpec((None, KV_CACHE, D_MODEL), lambda i: (i, 0, 0))
    return pl.pallas_call(
        _attn_sample_kernel, grid=(DEC_BATCH,),
        in_specs=[new, cache, cache, new, new,
                  pl.BlockSpec((N_HEADS, DEC_SEQ, KV_CACHE + DEC_SEQ), lambda i: (0, 0, 0))],
        out_specs=pl.BlockSpec((DEC_SEQ, D_MODEL), lambda i: (i, 0)),
        out_shape=jax.ShapeDtypeStruct((T_SAMPLE, D_MODEL), BF16),
        compiler_params=_cparams(("arbitrary",)),
        name="attn_sample")(q, cache_k, cache_v, k, v, table)


OPROJ_TM = 256


OPROJ_PROMPT_TILES = T_PROMPT // OPROJ_TM


def _oproj_kernel(op_ref, os_ref, wo_ref, x_ref, g_ref, b_ref, wr_ref, rb_ref, x1_ref, e_ref, gt_ref):
    o = jnp.where(pl.program_id(0) < OPROJ_PROMPT_TILES, op_ref[...], os_ref[...])
    y = jnp.dot(o, wo_ref[...], preferred_element_type=F32)
    _finish_mixer(x_ref[...], y, g_ref, b_ref, wr_ref, rb_ref, x1_ref, e_ref, gt_ref)


def _oproj_call(o_p, o_s, wo, x, g, b, wr_t, rb):
    row = pl.BlockSpec((OPROJ_TM, D_MODEL), lambda i: (i, 0))
    row_p = pl.BlockSpec((OPROJ_TM, D_MODEL), lambda i: (jnp.minimum(i, OPROJ_PROMPT_TILES - 1), 0))
    row_s = pl.BlockSpec((OPROJ_TM, D_MODEL), lambda i: (jnp.maximum(i - OPROJ_PROMPT_TILES, 0), 0))
    full = lambda a: pl.BlockSpec(a.shape, lambda i: (0,) * a.ndim)
    rt = pl.BlockSpec((TOP_K, OPROJ_TM), lambda i: (0, i))
    return pl.pallas_call(
        _oproj_kernel, grid=(T_ALL // OPROJ_TM,),
        in_specs=[row_p, row_s, full(wo), row, full(g), full(b), full(wr_t), full(rb)],
        out_specs=[row, rt, rt],
        out_shape=[jax.ShapeDtypeStruct((T_ALL, D_MODEL), F32),
                   jax.ShapeDtypeStruct((TOP_K, T_ALL), jnp.int32),
                   jax.ShapeDtypeStruct((TOP_K, T_ALL), F32)],
        compiler_params=_cparams(("arbitrary",)),
        name="oproj_ln_router")(o_p, o_s, wo, x, g, b, wr_t, rb)


def kernel(x_prompt, x_sample, state_pool, cache_k, cache_v, w_pool, pool_scale, w_q, w_o, rel_bias,
           w_k, w_v, ln_gain, ln_bias, w_router, router_bias, w_gate, w_up, w_down):
    wr_t = w_router.T
    rb = router_bias.reshape(N_EXPERTS, 1)
    wp_bf = w_pool.astype(BF16)
    xp0 = x_prompt.reshape(T_PROMPT, D_MODEL)
    xs0 = x_sample.reshape(T_SAMPLE, D_MODEL)

    pool_p, pool_s = [], []
    x = None
    for l in range(N_A_LAYERS):
        if l == 0:
            x_p, x_s, xs_off = xp0, xs0, 0
            in_p, in_s = x_prompt, x_sample
        else:
            x_p, x_s, xs_off = x, x, POOL_PROMPT_TILES
            in_p = x[:T_PROMPT].reshape(BATCH, SEQ, D_MODEL)
            in_s = x[T_PROMPT:].reshape(DEC_BATCH, DEC_SEQ, D_MODEL)
        pool_p.append(in_p[:, -POOL_CTX:])
        pool_s.append(in_s[:, -POOL_CTX:])
        x1, eidx, gate = _pool_layer(
            x_p, x_s, xs_off, state_pool[l], wp_bf[l], pool_scale[l].reshape(1, D_MODEL),
            ln_gain[l, 0].reshape(1, D_MODEL), ln_bias[l, 0].reshape(1, D_MODEL), wr_t, rb)
        x = _moe_layer(x1, eidx, gate, w_gate, w_up, w_down, l,
                       ln_gain[l, 1].reshape(1, D_MODEL), ln_bias[l, 1].reshape(1, D_MODEL))

    k, v = _proj_call(x, [w_k.astype(BF16), w_v.astype(BF16)], F32)
    table = _bias_table(rel_bias)
    ck = cache_k.reshape(DEC_BATCH, KV_CACHE, D_MODEL)
    cv = cache_v.reshape(DEC_BATCH, KV_CACHE, D_MODEL)
    for j in range(N_B_LAYERS):
        l = N_A_LAYERS + j
        (q,) = _proj_call(x, [w_q[j].astype(BF16)], BF16, scale=HEAD_DIM ** -0.5)
        o_p = _attn_prompt_call(q, k, v, table, j)
        o_s = _attn_sample_call(q, ck, cv, k, v,
                                table[j * N_HEADS:(j + 1) * N_HEADS, :DEC_SEQ, :KV_CACHE + DEC_SEQ])
        x1, eidx, gate = _oproj_call(
            o_p, o_s, w_o[j].astype(BF16), x, ln_gain[l, 0].reshape(1, D_MODEL),
            ln_bias[l, 0].reshape(1, D_MODEL), wr_t, rb)
        x = _moe_layer(x1, eidx, gate, w_gate, w_up, w_down, l,
                       ln_gain[l, 1].reshape(1, D_MODEL), ln_bias[l, 1].reshape(1, D_MODEL),
                       split=(l == DEPTH - 1))
    y_p, y_s = x

    keep = min(KV_WINDOW, SEQ)
    kp = k[:T_PROMPT].reshape(BATCH, SEQ, D_MODEL)[:, -keep:].reshape(BATCH, keep, N_HEADS, HEAD_DIM)
    vp = v[:T_PROMPT].reshape(BATCH, SEQ, D_MODEL)[:, -keep:].reshape(BATCH, keep, N_HEADS, HEAD_DIM)
    ks = k[T_PROMPT:].reshape(DEC_BATCH, DEC_SEQ, N_HEADS, HEAD_DIM)
    vs = v[T_PROMPT:].reshape(DEC_BATCH, DEC_SEQ, N_HEADS, HEAD_DIM)
    return (y_p.reshape(BATCH, SEQ, D_MODEL),
            y_s.reshape(DEC_BATCH, DEC_SEQ, D_MODEL),
            jnp.stack(pool_p, axis=0), kp, vp, jnp.stack(pool_s, axis=0), ks, vs)
```

```python
import functools

import numpy as np
import jax
import jax.numpy as jnp
from jax import lax
from jax.experimental import pallas as pl
from jax.experimental.pallas import tpu as pltpu

D_MODEL = 2048
BATCH = 2
SEQ = 4096
DEPTH = 4
DEC_BATCH = 16
DEC_SEQ = 32
PAST_LEN = 1024
CHUNK = 64
N_A_LAYERS = DEPTH // 2
N_B_LAYERS = DEPTH - N_A_LAYERS
POOL_WINDOWS = (2, 4, 8, 16)
POOL_GROUP_DIM = D_MODEL // len(POOL_WINDOWS)
POOL_CTX = max(POOL_WINDOWS) - 1
N_HEADS = 32
HEAD_DIM = D_MODEL // N_HEADS
N_LEFT_CHUNKS = 8
KV_WINDOW = N_LEFT_CHUNKS * CHUNK
REL_CLIP = 128
N_EXPERTS = 16
N_EXPERT_GROUPS = 4
EXPERTS_PER_GROUP = N_EXPERTS // N_EXPERT_GROUPS
TOP_K = 2
D_EXPERT = 1024
DEEPNORM_ALPHA = (2 * DEPTH) ** 0.25
LN_EPS = 1e-5
NEG_INF = -1e30

T_PROMPT = BATCH * SEQ
T_SAMPLE = DEC_BATCH * DEC_SEQ
T_ALL = T_PROMPT + T_SAMPLE
N_ASSIGN = T_ALL * TOP_K
BAND = KV_WINDOW + CHUNK
KV_CACHE = min(KV_WINDOW, PAST_LEN)
CTX_ROWS = POOL_CTX + 1

LANES = 128
VMEM_LIMIT = 60 * 1024 * 1024

MOE_BM = 256
MOE_CH = 256
MOE_NC = D_EXPERT // MOE_CH
MOE_NBLK = (N_ASSIGN + N_EXPERTS * (MOE_BM - 1)) // MOE_BM + 1
MOE_ROWS = MOE_NBLK * MOE_BM

BF16 = jnp.bfloat16
F32 = jnp.float32


def _cparams(sem):
    return pltpu.CompilerParams(dimension_semantics=sem, vmem_limit_bytes=VMEM_LIMIT)


def _layer_norm(v, g, b):
    mu = jnp.mean(v, axis=-1, keepdims=True)
    d = v - mu
    var = jnp.mean(d * d, axis=-1, keepdims=True)
    return d * lax.rsqrt(var + LN_EPS) * g + b


def _argmax_rows(rows):
    best = rows[0]
    idx = jnp.zeros(best.shape, jnp.int32)
    for i in range(1, len(rows)):
        better = rows[i] > best
        idx = jnp.where(better, i, idx)
        best = jnp.where(better, rows[i], best)
    return idx, best


def _select_rows(rows, idx):
    out = rows[-1]
    for i in range(len(rows) - 2, -1, -1):
        out = jnp.where(idx == i, rows[i], out)
    return out


def _route(x1, wr_t, rb):
    logits = lax.dot_general(wr_t.astype(BF16), x1.astype(BF16), (((1,), (1,)), ((), ())),
                             preferred_element_type=F32)
    m = jnp.max(logits, axis=0, keepdims=True)
    ex = jnp.exp(logits - m)
    probs = ex / jnp.sum(ex, axis=0, keepdims=True)
    sel = probs + rb
    srow = [sel[i:i + 1, :] for i in range(N_EXPERTS)]
    prow = [probs[i:i + 1, :] for i in range(N_EXPERTS)]
    gscore = []
    for g in range(N_EXPERT_GROUPS):
        a, b, c, d = srow[4 * g:4 * g + 4]
        hi1, lo1 = jnp.maximum(a, b), jnp.minimum(a, b)
        hi2, lo2 = jnp.maximum(c, d), jnp.minimum(c, d)
        top1 = jnp.maximum(hi1, hi2)
        top2 = jnp.maximum(jnp.minimum(hi1, hi2), jnp.maximum(lo1, lo2))
        gscore.append(top1 + top2)
    bg, _ = _argmax_rows(gscore)
    ig = [_select_rows([srow[4 * g + i] for g in range(N_EXPERT_GROUPS)], bg)
          for i in range(EXPERTS_PER_GROUP)]
    pg = [_select_rows([prow[4 * g + i] for g in range(N_EXPERT_GROUPS)], bg)
          for i in range(EXPERTS_PER_GROUP)]
    l0, _ = _argmax_rows(ig)
    masked = [jnp.where(l0 == i, -jnp.inf, ig[i]) for i in range(EXPERTS_PER_GROUP)]
    l1, _ = _argmax_rows(masked)
    g0 = _select_rows(pg, l0)
    g1 = _select_rows(pg, l1)
    den = g0 + g1
    eidx = jnp.concatenate([bg * EXPERTS_PER_GROUP + l0, bg * EXPERTS_PER_GROUP + l1], axis=0)
    gate = jnp.concatenate([g0 / den, g1 / den], axis=0)
    return eidx, gate


def _finish_mixer(x, y, g_ref, b_ref, wr_ref, rb_ref, x1_ref, e_ref, gt_ref):
    x1 = _layer_norm(DEEPNORM_ALPHA * x + y, g_ref[...], b_ref[...])
    x1_ref[...] = x1
    eidx, gate = _route(x1, wr_ref[...], rb_ref[...])
    e_ref[...] = eidx
    gt_ref[...] = gate


def _pool_tile(i, x_ref, ctx_ref, wp_ref, sc_ref, g_ref, b_ref, wr_ref, rb_ref, x1_ref, e_ref, gt_ref,
               xp, ybuf, *, n_seg, seg_len, tiles_per_seq):
    rows = n_seg * seg_len
    stride = CTX_ROWS + seg_len
    for s in range(n_seg):
        c = ctx_ref[s * CTX_ROWS:(s + 1) * CTX_ROWS, :]
        if tiles_per_seq:
            c = jnp.where(i % tiles_per_seq == 0, 0.0, c)
        xp[s * stride:s * stride + CTX_ROWS, :] = c
        xp[s * stride + CTX_ROWS:(s + 1) * stride, :] = x_ref[s * seg_len:(s + 1) * seg_len, :]
    r = lax.broadcasted_iota(jnp.int32, (rows, 1), 0)
    if tiles_per_seq:
        pos = (i % tiles_per_seq) * rows + r
    else:
        pos = PAST_LEN + r % seg_len
    x = x_ref[...]
    for g, w in enumerate(POOL_WINDOWS):
        c0, c1 = g * POOL_GROUP_DIM, (g + 1) * POOL_GROUP_DIM
        segs = []
        for s in range(n_seg):
            base = s * stride + CTX_ROWS
            acc = xp[base:base + seg_len, c0:c1]
            for j in range(1, w):
                acc = acc + xp[base - j:base - j + seg_len, c0:c1]
            segs.append(acc)
        ws = segs[0] if n_seg == 1 else jnp.concatenate(segs, axis=0)
        cnt = jnp.minimum(w, pos + 1).astype(F32)
        p = ws / cnt - x[:, c0:c1]
        yg = jnp.dot(p.astype(BF16), wp_ref[g], preferred_element_type=F32)
        ybuf[:, c0:c1] = yg * sc_ref[:, c0:c1]
    _finish_mixer(x, ybuf[...], g_ref, b_ref, wr_ref, rb_ref, x1_ref, e_ref, gt_ref)


POOL_TM = 128
POOL_PROMPT_TILES = T_PROMPT // POOL_TM
POOL_SAMPLE_SEGS = POOL_TM // DEC_SEQ


def _pool_kernel(xp_ref, cp_ref, xs_ref, cs_ref, *rest):
    i = pl.program_id(0)

    @pl.when(i < POOL_PROMPT_TILES)
    def _prompt():
        _pool_tile(i, xp_ref, cp_ref, *rest, n_seg=1, seg_len=POOL_TM, tiles_per_seq=SEQ // POOL_TM)

    @pl.when(i >= POOL_PROMPT_TILES)
    def _sample():
        _pool_tile(i, xs_ref, cs_ref, *rest, n_seg=POOL_SAMPLE_SEGS, seg_len=DEC_SEQ, tiles_per_seq=0)


def _pool_layer(x_p, x_s, xs_blk_off, state_l, wp, sc, g, b, wr_t, rb):
    ctx_s = jnp.pad(state_l, ((0, 0), (1, 0), (0, 0))).reshape(DEC_BATCH * CTX_ROWS, D_MODEL)
    ctx_blocks = POOL_TM // CTX_ROWS
    last = POOL_PROMPT_TILES - 1
    full = lambda a: pl.BlockSpec(a.shape, lambda i: (0,) * a.ndim)
    in_specs = [
        pl.BlockSpec((POOL_TM, D_MODEL), lambda i: (jnp.minimum(i, last), 0)),
        pl.BlockSpec((CTX_ROWS, D_MODEL), lambda i: (jnp.maximum(jnp.minimum(i, last) * ctx_blocks - 1, 0), 0)),
        pl.BlockSpec((POOL_TM, D_MODEL), lambda i: (xs_blk_off + jnp.maximum(i - POOL_PROMPT_TILES, 0), 0)),
        pl.BlockSpec((POOL_SAMPLE_SEGS * CTX_ROWS, D_MODEL), lambda i: (jnp.maximum(i - POOL_PROMPT_TILES, 0), 0)),
        full(wp), full(sc), full(g), full(b), full(wr_t), full(rb),
    ]
    out_specs = [
        pl.BlockSpec((POOL_TM, D_MODEL), lambda i: (i, 0)),
        pl.BlockSpec((TOP_K, POOL_TM), lambda i: (0, i)),
        pl.BlockSpec((TOP_K, POOL_TM), lambda i: (0, i)),
    ]
    out_shape = [jax.ShapeDtypeStruct((T_ALL, D_MODEL), F32),
                 jax.ShapeDtypeStruct((TOP_K, T_ALL), jnp.int32),
                 jax.ShapeDtypeStruct((TOP_K, T_ALL), F32)]
    xp_rows = max(CTX_ROWS + POOL_TM, POOL_SAMPLE_SEGS * (CTX_ROWS + DEC_SEQ))
    return pl.pallas_call(
        _pool_kernel, grid=(T_ALL // POOL_TM,), in_specs=in_specs, out_specs=out_specs, out_shape=out_shape,
        scratch_shapes=[pltpu.VMEM((xp_rows, D_MODEL), F32),
                        pltpu.VMEM((POOL_TM, D_MODEL), F32)],
        compiler_params=_cparams(("arbitrary",)),
        name="pool_ln_router")(x_p, x_p, x_s, ctx_s, wp, sc, g, b, wr_t, rb)


def _moe_plan(eidx):
    e = eidx.reshape(N_ASSIGN)
    ids = jnp.arange(N_ASSIGN, dtype=jnp.int32)
    oh = (e[:, None] == jnp.arange(N_EXPERTS, dtype=jnp.int32)[None, :]).astype(jnp.int32)
    csum = jnp.cumsum(oh, axis=0)
    counts = csum[-1]
    rank = jnp.sum(csum * oh, axis=1) - 1
    nblk = (counts + MOE_BM - 1) // MOE_BM
    blk_off = jnp.cumsum(nblk) - nblk
    pos = jnp.sum(oh * blk_off[None, :], axis=1) * MOE_BM + rank
    spare = N_ASSIGN + jnp.arange(MOE_ROWS, dtype=jnp.int32) % MOE_BM
    dst = spare.at[pos].set(ids)
    src = jnp.where(dst < N_ASSIGN, dst % T_ALL, 0)
    blk_end = blk_off + nblk
    n_act = blk_end[-1]
    b = jnp.minimum(jnp.arange(MOE_NBLK, dtype=jnp.int32), n_act - 1)
    be = jnp.minimum(jnp.sum((blk_end[None, :] <= b[:, None]).astype(jnp.int32), axis=1), N_EXPERTS - 1)
    bi = b - blk_off[be]
    used = nblk > 0
    run = jnp.cumsum(used.astype(jnp.int32)) - used.astype(jnp.int32)
    nxt, cur = [None] * N_EXPERTS, jnp.int32(-1)
    for x in reversed(range(N_EXPERTS)):
        nxt[x] = cur
        cur = jnp.where(used[x], x, cur)
    prv, cur = [None] * N_EXPERTS, jnp.int32(-1)
    for x in range(N_EXPERTS):
        prv[x] = cur
        cur = jnp.where(used[x], x, cur)
    nxt, prv = jnp.stack(nxt), jnp.stack(prv)
    nxt2 = jnp.where(nxt >= 0, nxt[jnp.maximum(nxt, 0)], -1)
    lprev = jnp.where(prv >= 0, nblk[jnp.maximum(prv, 0)], MOE_NC)
    i32 = lambda a: a.astype(jnp.int32)
    return (i32(be), i32(bi), i32(run[be] % 2), i32(lprev[be]), i32(nxt[be]), i32(nxt2[be]),
            i32(n_act).reshape(1), i32(src), i32(dst))


MOE_DMA_UNROLL = 8


def _moe_kernel(be_ref, bi_ref, slot_ref, lprev_ref, enext_ref, enext2_ref, nact_ref, srctok, dstrow,
                x_hbm, wg_hbm, wu_hbm, wd_hbm, y_hbm,
                wgb, wub, wdb, wg_st, wu_st, wd_st, xstage, xbf, ybuf, gsem, ssem, wsem, *, layer):
    b = pl.program_id(0)
    n_act = nact_ref[0]
    e, i, p = be_ref[b], bi_ref[b], slot_ref[b]
    e_next, e_next2 = enext_ref[b], enext2_ref[b]
    cur = b % 2
    nxt = 1 - cur

    def piece_copies(ex, q):
        s = q % 2
        cols = pl.ds(q * MOE_CH, MOE_CH)
        return (pltpu.make_async_copy(wg_hbm.at[layer, ex, :, cols], wg_st.at[s], wsem.at[s]),
                pltpu.make_async_copy(wu_hbm.at[layer, ex, :, cols], wu_st.at[s], wsem.at[s]),
                pltpu.make_async_copy(wd_hbm.at[layer, ex, cols, :], wd_st.at[s], wsem.at[s]))

    def start_piece(ex, q):
        for cp in piece_copies(ex, q):
            cp.start()

    def do_piece(ex, q, slot, ex_after):
        for cp in piece_copies(ex, q):
            cp.wait()
        s = q % 2
        wgb[slot, q] = wg_st[s].astype(BF16)
        wub[slot, q] = wu_st[s].astype(BF16)
        wdb[slot, q] = wd_st[s].astype(BF16)
        if q + 1 < MOE_NC:
            start_piece(ex, q + 1)
        else:
            @pl.when(ex_after >= 0)
            def _():
                start_piece(ex_after, 0)

    def gather_row(t, slot, r):
        return pltpu.make_async_copy(x_hbm.at[pl.ds(t, 1)], xstage.at[slot, pl.ds(r, 1)], gsem.at[slot])

    def scatter_row(slot, r, d):
        return pltpu.make_async_copy(ybuf.at[slot, pl.ds(r, 1)], y_hbm.at[pl.ds(d, 1)], ssem)

    def block_gather_wait(slot):
        pltpu.make_async_copy(x_hbm.at[pl.ds(0, MOE_BM)], xstage.at[slot], gsem.at[slot]).wait()

    def block_scatter_wait(slot):
        pltpu.make_async_copy(ybuf.at[slot], y_hbm.at[pl.ds(0, MOE_BM)], ssem).wait()

    @pl.when(b == 0)
    def _prologue():
        zeros = jnp.zeros((MOE_BM, D_MODEL), F32)
        ybuf[0] = zeros
        ybuf[1] = zeros
        spare = pltpu.make_async_copy(ybuf.at[0], y_hbm.at[pl.ds(N_ASSIGN, MOE_BM)], ssem)
        spare.start()
        spare.wait()
        start_piece(e, 0)
        for q in range(MOE_NC):
            do_piece(e, q, p, e_next)

        def issue(g, carry):
            for j in range(MOE_DMA_UNROLL):
                r = g * MOE_DMA_UNROLL + j
                gather_row(srctok[r], 0, r).start()
            return carry
        lax.fori_loop(0, MOE_BM // MOE_DMA_UNROLL, issue, 0)

    @pl.when(b < n_act)
    def _block():
        block_gather_wait(cur)

        @pl.when(b > 0)
        def _():
            block_scatter_wait(nxt)

        @pl.when(jnp.logical_and(b > 0, i == 0))
        def _finish_own_weights():
            lprev = lprev_ref[b]
            for q in range(1, MOE_NC):
                @pl.when(q >= lprev)
                def _():
                    do_piece(e, q, p, e_next)

        next_rows = (b + 1) * MOE_BM
        prev_rows = jnp.maximum(b - 1, 0) * MOE_BM
        for j in range(MOE_BM):
            gather_row(srctok[next_rows + j], nxt, j).start()
        for j in range(MOE_BM):
            scatter_row(nxt, j, dstrow[prev_rows + j]).start()
        xbf[...] = xstage[cur].astype(BF16)
        for c in range(MOE_NC):
            x = xbf[...]
            a = jnp.dot(x, wgb[p, c], preferred_element_type=F32)
            u = jnp.dot(x, wub[p, c], preferred_element_type=F32)
            h = (a * jax.nn.sigmoid(a)) * u
            y = jnp.dot(h.astype(BF16), wdb[p, c], preferred_element_type=F32)
            if c == 0:
                ybuf[cur] = y
            else:
                ybuf[cur] += y

        @pl.when(e_next >= 0)
        def _next_weights():
            for q in range(MOE_NC):
                @pl.when(i == q)
                def _():
                    do_piece(e_next, q, 1 - p, e_next2)

    @pl.when(b == n_act)
    def _flush():
        block_gather_wait(cur)
        block_scatter_wait(nxt)
        last_rows = (b - 1) * MOE_BM

        def issue(g, carry):
            for j in range(MOE_DMA_UNROLL):
                r = g * MOE_DMA_UNROLL + j
                scatter_row(nxt, r, dstrow[last_rows + r]).start()
            return carry
        lax.fori_loop(0, MOE_BM // MOE_DMA_UNROLL, issue, 0)
        block_scatter_wait(nxt)


def _moe_call(x1, plan, wg, wu, wd, l):
    any_spec = pl.BlockSpec(memory_space=pl.ANY)
    grid_spec = pltpu.PrefetchScalarGridSpec(
        num_scalar_prefetch=len(plan),
        grid=(MOE_NBLK,),
        in_specs=[any_spec, any_spec, any_spec, any_spec],
        out_specs=any_spec,
        scratch_shapes=[
            pltpu.VMEM((2, MOE_NC, D_MODEL, MOE_CH), BF16),
            pltpu.VMEM((2, MOE_NC, D_MODEL, MOE_CH), BF16),
            pltpu.VMEM((2, MOE_NC, MOE_CH, D_MODEL), BF16),
            pltpu.VMEM((2, D_MODEL, MOE_CH), F32),
            pltpu.VMEM((2, D_MODEL, MOE_CH), F32),
            pltpu.VMEM((2, MOE_CH, D_MODEL), F32),
            pltpu.VMEM((2, MOE_BM, D_MODEL), F32),
            pltpu.VMEM((MOE_BM, D_MODEL), BF16),
            pltpu.VMEM((2, MOE_BM, D_MODEL), F32),
            pltpu.SemaphoreType.DMA((2,)),
            pltpu.SemaphoreType.DMA,
            pltpu.SemaphoreType.DMA((2,)),
        ])
    return pl.pallas_call(
        functools.partial(_moe_kernel, layer=l), grid_spec=grid_spec,
        out_shape=jax.ShapeDtypeStruct((N_ASSIGN + MOE_BM, D_MODEL), F32),
        compiler_params=_cparams(("arbitrary",)),
        name="moe_experts")(*plan, x1, wg, wu, wd)


COMBINE_TM = 256


COMBINE_PROMPT_TILES = T_PROMPT // COMBINE_TM


def _combine_kernel(x_ref, y0_ref, y1_ref, gt_ref, g_ref, b_ref, *o_refs):
    gt = gt_ref[...]
    m = gt[:, 0:1] * y0_ref[...] + gt[:, 1:2] * y1_ref[...]
    out = _layer_norm(DEEPNORM_ALPHA * x_ref[...] + m, g_ref[...], b_ref[...])
    if len(o_refs) == 1:
        o_refs[0][...] = out
    else:
        i = pl.program_id(0)

        @pl.when(i < COMBINE_PROMPT_TILES)
        def _prompt():
            o_refs[0][...] = out

        @pl.when(i >= COMBINE_PROMPT_TILES)
        def _sample():
            o_refs[1][...] = out


def _combine_call(x1, y2, gate_t, g, b, split):
    row = pl.BlockSpec((COMBINE_TM, D_MODEL), lambda i: (i, 0))
    vec = pl.BlockSpec((1, D_MODEL), lambda i: (0, 0))
    if split:
        out_specs = [
            pl.BlockSpec((COMBINE_TM, D_MODEL), lambda i: (jnp.minimum(i, COMBINE_PROMPT_TILES - 1), 0)),
            pl.BlockSpec((COMBINE_TM, D_MODEL), lambda i: (jnp.maximum(i - COMBINE_PROMPT_TILES, 0), 0))]
        out_shape = [jax.ShapeDtypeStruct((T_PROMPT, D_MODEL), F32),
                     jax.ShapeDtypeStruct((T_SAMPLE, D_MODEL), F32)]
    else:
        out_specs = row
        out_shape = jax.ShapeDtypeStruct((T_ALL, D_MODEL), F32)
    return pl.pallas_call(
        _combine_kernel, grid=(T_ALL // COMBINE_TM,),
        in_specs=[row, row,
                  pl.BlockSpec((COMBINE_TM, D_MODEL), lambda i: (T_ALL // COMBINE_TM + i, 0)),
                  pl.BlockSpec((COMBINE_TM, TOP_K), lambda i: (i, 0)),
                  vec, vec],
        out_specs=out_specs, out_shape=out_shape,
        compiler_params=_cparams(("arbitrary",)),
        name="combine_ln")(x1, y2, y2, gate_t, g, b)


def _moe_layer(x1, eidx, gate, wg, wu, wd, l, g, b, split=False):
    y2 = _moe_call(x1, _moe_plan(eidx), wg, wu, wd, l)
    return _combine_call(x1, y2, gate.T, g, b, split)


PROJ_TM = 512


def _proj_kernel(*refs, n_w, scale):
    x = refs[0][...].astype(BF16)
    for k in range(n_w):
        o_ref = refs[1 + n_w + k]
        y = jnp.dot(x, refs[1 + k][...], preferred_element_type=F32)
        if scale != 1.0:
            y = y * scale
        o_ref[...] = y.astype(o_ref.dtype)


def _proj_call(x, ws, out_dtype, scale=1.0):
    row = pl.BlockSpec((PROJ_TM, D_MODEL), lambda i: (i, 0))
    wspec = pl.BlockSpec((D_MODEL, D_MODEL), lambda i: (0, 0))
    n_w = len(ws)
    return pl.pallas_call(
        functools.partial(_proj_kernel, n_w=n_w, scale=scale), grid=(T_ALL // PROJ_TM,),
        in_specs=[row] + [wspec] * n_w, out_specs=[row] * n_w,
        out_shape=[jax.ShapeDtypeStruct((T_ALL, D_MODEL), out_dtype)] * n_w,
        compiler_params=_cparams(("arbitrary",)),
        name="proj")(x, *ws)


ATT_TQ = 256
ATT_TK = KV_WINDOW + ATT_TQ
REL_PAD = 384
REL_MASKED = 2 * REL_CLIP + 1


BIAS_ROWS = 8
BIAS_HEADS = N_B_LAYERS * N_HEADS


def _bias_kernel(rb_ref, idx_ref, o_ref):
    rb = rb_ref[...]
    hi = rb.astype(BF16)
    rest = rb - hi.astype(F32)
    mid = rest.astype(BF16)
    lo = (rest - mid.astype(F32)).astype(BF16)
    r = lax.broadcasted_iota(jnp.int32, (REL_PAD, ATT_TK), 0)
    for i in range(BIAS_ROWS):
        onehot = jnp.where(r == idx_ref[i:i + 1, :], 1.0, 0.0).astype(BF16)
        o_ref[:, i, :] = (jnp.dot(hi, onehot, preferred_element_type=F32)
                          + jnp.dot(mid, onehot, preferred_element_type=F32)
                          + jnp.dot(lo, onehot, preferred_element_type=F32))


def _bias_table(rel_bias):
    i = np.arange(ATT_TQ)[:, None]
    r = np.arange(ATT_TK)[None, :]
    lo = (i // CHUNK) * CHUNK
    idx = np.clip(i - r + KV_WINDOW, -REL_CLIP, REL_CLIP) + REL_CLIP
    idx = jnp.asarray(np.where((r >= lo) & (r < lo + BAND), idx, REL_MASKED).astype(np.int32))
    n_rel = rel_bias.shape[-1]
    rb = jnp.pad(rel_bias, ((0, 0), (0, 0), (0, REL_PAD - n_rel)))
    rb = rb.at[:, :, REL_MASKED].set(NEG_INF).reshape(BIAS_HEADS, REL_PAD)
    return pl.pallas_call(
        _bias_kernel, grid=(ATT_TQ // BIAS_ROWS,),
        in_specs=[pl.BlockSpec((BIAS_HEADS, REL_PAD), lambda t: (0, 0)),
                  pl.BlockSpec((BIAS_ROWS, ATT_TK), lambda t: (t, 0))],
        out_specs=pl.BlockSpec((BIAS_HEADS, BIAS_ROWS, ATT_TK), lambda t: (0, t, 0)),
        out_shape=jax.ShapeDtypeStruct((BIAS_HEADS, ATT_TQ, ATT_TK), F32),
        compiler_params=_cparams(("arbitrary",)),
        name="bias_table")(rb, idx)


def _head_pair_attention(qc, kb, vb, bias0, bias1, first_valid):
    first = lax.broadcasted_iota(jnp.int32, (1, LANES), 1) < HEAD_DIM
    outs = []
    for h, bias in enumerate((bias0, bias1)):
        qh = jnp.where(first if h == 0 else jnp.logical_not(first), qc, jnp.zeros_like(qc))
        s = lax.dot_general(qh, kb, (((1,), (1,)), ((), ())), preferred_element_type=F32) + bias
        if first_valid is not None:
            kidx = lax.broadcasted_iota(jnp.int32, (1, s.shape[1]), 1)
            s = jnp.where(kidx >= first_valid, s, NEG_INF)
        ex = jnp.exp(s - jnp.max(s, axis=-1, keepdims=True))
        den = jnp.sum(ex, axis=-1, keepdims=True)
        outs.append(jnp.dot(ex.astype(BF16), vb, preferred_element_type=F32) / den)
    return jnp.where(first, outs[0], outs[1])


ATT_TILES = SEQ // ATT_TQ
ATT_EARLY_TILES = KV_WINDOW // ATT_TQ


def _attn_prompt_kernel(q_ref, k_ref, v_ref, b_ref, o_ref, kp, vp):
    kp[0:KV_WINDOW, :] = jnp.zeros((KV_WINDOW, LANES), BF16)
    vp[0:KV_WINDOW, :] = jnp.zeros((KV_WINDOW, LANES), BF16)
    kp[KV_WINDOW:, :] = k_ref[...].astype(BF16)
    vp[KV_WINDOW:, :] = v_ref[...].astype(BF16)

    def tile(r0, first_valid):
        o = _head_pair_attention(q_ref[pl.ds(r0, ATT_TQ), :], kp[pl.ds(r0, ATT_TK), :],
                                 vp[pl.ds(r0, ATT_TK), :], b_ref[0], b_ref[1], first_valid)
        o_ref[pl.ds(r0, ATT_TQ), :] = o.astype(o_ref.dtype)

    for t in range(ATT_EARLY_TILES):
        tile(t * ATT_TQ, KV_WINDOW - t * ATT_TQ)

    def body(t, carry):
        tile(pl.multiple_of(2 * t * ATT_TQ, ATT_TQ), None)
        tile(pl.multiple_of((2 * t + 1) * ATT_TQ, ATT_TQ), None)
        return carry
    lax.fori_loop(ATT_EARLY_TILES // 2, ATT_TILES // 2, body, 0)


def _attn_prompt_call(q, k, v, table, j):
    col = pl.BlockSpec((SEQ, LANES), lambda b, hp: (b, hp))
    return pl.pallas_call(
        _attn_prompt_kernel, grid=(BATCH, N_HEADS // 2),
        in_specs=[col, col, col,
                  pl.BlockSpec((2, ATT_TQ, ATT_TK), lambda b, hp: (j * (N_HEADS // 2) + hp, 0, 0))],
        out_specs=col,
        out_shape=jax.ShapeDtypeStruct((T_PROMPT, D_MODEL), BF16),
        scratch_shapes=[pltpu.VMEM((KV_WINDOW + SEQ, LANES), BF16),
                        pltpu.VMEM((KV_WINDOW + SEQ, LANES), BF16)],
        compiler_params=_cparams(("arbitrary", "arbitrary")),
        name="attn_prompt")(q, k, v, table)


def _attn_sample_kernel(q_ref, kc_ref, vc_ref, kn_ref, vn_ref, b_ref, o_ref):
    for hp in range(N_HEADS // 2):
        cols = slice(hp * LANES, (hp + 1) * LANES)
        kb = jnp.concatenate([kc_ref[:, cols], kn_ref[:, cols]], axis=0).astype(BF16)
        vb = jnp.concatenate([vc_ref[:, cols], vn_ref[:, cols]], axis=0).astype(BF16)
        o = _head_pair_attention(q_ref[:, cols], kb, vb, b_ref[2 * hp], b_ref[2 * hp + 1], None)
        o_ref[:, cols] = o.astype(o_ref.dtype)


def _attn_sample_call(q, cache_k, cache_v, k, v, table):
    off = T_PROMPT // DEC_SEQ
    new = pl.BlockSpec((DEC_SEQ, D_MODEL), lambda i: (off + i, 0))
    cache = pl.BlockSpec((None, KV_CACHE, D_MODEL), lambda i: (i, 0, 0))
    return pl.pallas_call(
        _attn_sample_kernel, grid=(DEC_BATCH,),
        in_specs=[new, cache, cache, new, new,
                  pl.BlockSpec((N_HEADS, DEC_SEQ, KV_CACHE + DEC_SEQ), lambda i: (0, 0, 0))],
        out_specs=pl.BlockSpec((DEC_SEQ, D_MODEL), lambda i: (i, 0)),
        out_shape=jax.ShapeDtypeStruct((T_SAMPLE, D_MODEL), BF16),
        compiler_params=_cparams(("arbitrary",)),
        name="attn_sample")(q, cache_k, cache_v, k, v, table)


OPROJ_TM = 256


OPROJ_PROMPT_TILES = T_PROMPT // OPROJ_TM


def _oproj_kernel(op_ref, os_ref, wo_ref, x_ref, g_ref, b_ref, wr_ref, rb_ref, x1_ref, e_ref, gt_ref):
    o = jnp.where(pl.program_id(0) < OPROJ_PROMPT_TILES, op_ref[...], os_ref[...])
    y = jnp.dot(o, wo_ref[...], preferred_element_type=F32)
    _finish_mixer(x_ref[...], y, g_ref, b_ref, wr_ref, rb_ref, x1_ref, e_ref, gt_ref)


def _oproj_call(o_p, o_s, wo, x, g, b, wr_t, rb):
    row = pl.BlockSpec((OPROJ_TM, D_MODEL), lambda i: (i, 0))
    row_p = pl.BlockSpec((OPROJ_TM, D_MODEL), lambda i: (jnp.minimum(i, OPROJ_PROMPT_TILES - 1), 0))
    row_s = pl.BlockSpec((OPROJ_TM, D_MODEL), lambda i: (jnp.maximum(i - OPROJ_PROMPT_TILES, 0), 0))
    full = lambda a: pl.BlockSpec(a.shape, lambda i: (0,) * a.ndim)
    rt = pl.BlockSpec((TOP_K, OPROJ_TM), lambda i: (0, i))
    return pl.pallas_call(
        _oproj_kernel, grid=(T_ALL // OPROJ_TM,),
        in_specs=[row_p, row_s, full(wo), row, full(g), full(b), full(wr_t), full(rb)],
        out_specs=[row, rt, rt],
        out_shape=[jax.ShapeDtypeStruct((T_ALL, D_MODEL), F32),
                   jax.ShapeDtypeStruct((TOP_K, T_ALL), jnp.int32),
                   jax.ShapeDtypeStruct((TOP_K, T_ALL), F32)],
        compiler_params=_cparams(("arbitrary",)),
        name="oproj_ln_router")(o_p, o_s, wo, x, g, b, wr_t, rb)


def kernel(x_prompt, x_sample, state_pool, cache_k, cache_v, w_pool, pool_scale, w_q, w_o, rel_bias,
           w_k, w_v, ln_gain, ln_bias, w_router, router_bias, w_gate, w_up, w_down):
    wr_t = w_router.T
    rb = router_bias.reshape(N_EXPERTS, 1)
    wp_bf = w_pool.astype(BF16)
    xp0 = x_prompt.reshape(T_PROMPT, D_MODEL)
    xs0 = x_sample.reshape(T_SAMPLE, D_MODEL)

    pool_p, pool_s = [], []
    x = None
    for l in range(N_A_LAYERS):
        if l == 0:
            x_p, x_s, xs_off = xp0, xs0, 0
        else:
            x_p, x_s, xs_off = x, x, POOL_PROMPT_TILES
        pool_p.append(jnp.stack([x_p[(s + 1) * SEQ - POOL_CTX:(s + 1) * SEQ] for s in range(BATCH)]))
        first_s = xs_off * POOL_TM
        pool_s.append(jnp.stack([x_s[first_s + (s + 1) * DEC_SEQ - POOL_CTX:first_s + (s + 1) * DEC_SEQ]
                                 for s in range(DEC_BATCH)]))
        x1, eidx, gate = _pool_layer(
            x_p, x_s, xs_off, state_pool[l], wp_bf[l], pool_scale[l].reshape(1, D_MODEL),
            ln_gain[l, 0].reshape(1, D_MODEL), ln_bias[l, 0].reshape(1, D_MODEL), wr_t, rb)
        x = _moe_layer(x1, eidx, gate, w_gate, w_up, w_down, l,
                       ln_gain[l, 1].reshape(1, D_MODEL), ln_bias[l, 1].reshape(1, D_MODEL))

    k, v = _proj_call(x, [w_k.astype(BF16), w_v.astype(BF16)], F32)
    table = _bias_table(rel_bias)
    ck = cache_k.reshape(DEC_BATCH, KV_CACHE, D_MODEL)
    cv = cache_v.reshape(DEC_BATCH, KV_CACHE, D_MODEL)
    for j in range(N_B_LAYERS):
        l = N_A_LAYERS + j
        (q,) = _proj_call(x, [w_q[j].astype(BF16)], BF16, scale=HEAD_DIM ** -0.5)
        o_p = _attn_prompt_call(q, k, v, table, j)
        o_s = _attn_sample_call(q, ck, cv, k, v,
                                table[j * N_HEADS:(j + 1) * N_HEADS, :DEC_SEQ, :KV_CACHE + DEC_SEQ])
        x1, eidx, gate = _oproj_call(
            o_p, o_s, w_o[j].astype(BF16), x, ln_gain[l, 0].reshape(1, D_MODEL),
            ln_bias[l, 0].reshape(1, D_MODEL), wr_t, rb)
        x = _moe_layer(x1, eidx, gate, w_gate, w_up, w_down, l,
                       ln_gain[l, 1].reshape(1, D_MODEL), ln_bias[l, 1].reshape(1, D_MODEL),
                       split=(l == DEPTH - 1))
    y_p, y_s = x

    keep = min(KV_WINDOW, SEQ)
    tail = lambda a: jnp.stack([a[(s + 1) * SEQ - keep:(s + 1) * SEQ] for s in range(BATCH)])
    kp = tail(k).reshape(BATCH, keep, N_HEADS, HEAD_DIM)
    vp = tail(v).reshape(BATCH, keep, N_HEADS, HEAD_DIM)
    ks = k[T_PROMPT:].reshape(DEC_BATCH, DEC_SEQ, N_HEADS, HEAD_DIM)
    vs = v[T_PROMPT:].reshape(DEC_BATCH, DEC_SEQ, N_HEADS, HEAD_DIM)
    return (y_p.reshape(BATCH, SEQ, D_MODEL),
            y_s.reshape(DEC_BATCH, DEC_SEQ, D_MODEL),
            jnp.stack(pool_p, axis=0), kp, vp, jnp.stack(pool_s, axis=0), ks, vs)
```
